```python
import jax, jax.numpy as jnp
from jax import lax
import numpy as np

D_MODEL = 4096
BATCH = 4
SEQ = 4096
DEPTH = 1
DEC_BATCH = 16
DEC_SEQ = 32
PAST_LEN = 4096

CHUNK = 64
SCAN_BLOCK = CHUNK // 4
D_MIX = D_MODEL
D_CONV = D_MIX // 2
CONV_GROUPS = 16
CONV_W = 3
D_HG = D_MIX - D_CONV
HG_DK = 128
HG_DV = 128
HG_HEADS = D_HG // HG_DK
N_IN = 3 * D_CONV + 4 * D_HG
D_FF = ((8 * D_MODEL // 3 + 255) // 256) * 256
NORM_EPS = 1e-6

kernel_name = "hybrid_shortconv_hgrn2_stream_step"


def rmsnorm(x, w):
    xf = x.astype(jnp.float32)
    y = xf * lax.rsqrt(jnp.mean(xf * xf, axis=-1, keepdims=True) + NORM_EPS)
    return (y * w.astype(jnp.float32)).astype(x.dtype)


def _to_blocks(a, n_blk):
    b, t, h, e = a.shape
    a = jnp.pad(a, ((0, 0), (0, n_blk * SCAN_BLOCK - t), (0, 0), (0, 0)))
    return a.reshape(b, n_blk, SCAN_BLOCK, h, e).transpose(1, 0, 3, 2, 4)


def hgrn2_scan(q, k, v, g, s0):
    b, t = q.shape[0], q.shape[1]
    n_blk = -(-t // SCAN_BLOCK)
    qb, kb, vb, gb = (_to_blocks(a, n_blk) for a in (q, k, v, g))
    mask = jnp.tril(jnp.ones((SCAN_BLOCK, SCAN_BLOCK), jnp.float32))
    mid = SCAN_BLOCK // 2

    def step(s, blk):
        qc, kc, vc, gc = blk
        lc = jnp.cumsum(gc, axis=-2)
        ref = lc[..., mid - 1:mid, :]
        qd = qc * jnp.exp(lc - ref)
        kd = kc * jnp.exp(ref - lc)
        att = jnp.einsum('bhtk,bhsk->bhts', qd, kd) * mask
        o = (jnp.einsum('bhts,bhsv->bhtv', att, vc)
             + jnp.einsum('bhtk,bhkv->bhtv', qc * jnp.exp(lc), s))
        llast = lc[..., -1:, :]
        s = (jnp.exp(llast)[..., 0, :, None] * s
             + jnp.einsum('bhsk,bhsv->bhkv', kc * jnp.exp(llast - lc), vc))
        return s, o

    s_new, o = lax.scan(step, s0, (qb, kb, vb, gb))
    o = o.transpose(1, 0, 3, 2, 4).reshape(b, n_blk * SCAN_BLOCK, HG_HEADS, HG_DV)[:, :t]
    return o, s_new


def layer(x, conv_buf, s0, lb, norm_mix, w_in, conv_w, hg_norm, w_out,
          norm_ffn, w_gate, w_up, w_down):
    b, t, _ = x.shape
    h = rmsnorm(x, norm_mix)
    proj = h @ w_in
    cuts = [D_CONV, 2 * D_CONV, 3 * D_CONV, 3 * D_CONV + D_HG,
            3 * D_CONV + 2 * D_HG, 3 * D_CONV + 3 * D_HG]
    b_gate, c_gate, u, q, fz, i_in, og = jnp.split(proj, cuts, axis=-1)

    cu = c_gate * u
    up = jnp.concatenate([conv_buf.astype(cu.dtype), cu], axis=1)
    conv = (conv_w[0] * up[:, 0:t] + conv_w[1] * up[:, 1:t + 1]
            + conv_w[2] * up[:, 2:t + 2])
    y_a = b_gate * conv
    new_buf = up[:, -(CONV_W - 1):]

    fzf = fz.astype(jnp.float32)
    f = lb + (1.0 - lb) * jax.nn.sigmoid(fzf)
    g_log = jnp.log(f)
    k = (1.0 - lb) * jax.nn.sigmoid(-fzf)
    qf = jax.nn.silu(q.astype(jnp.float32))
    heads = lambda a: a.reshape(b, t, HG_HEADS, -1)
    o, s_new = hgrn2_scan(heads(qf), heads(k), heads(i_in.astype(jnp.float32)),
                          heads(g_log), s0.astype(jnp.float32))
    o = rmsnorm(o, hg_norm) * jax.nn.silu(heads(og.astype(jnp.float32)))
    y_b = o.reshape(b, t, D_HG).astype(x.dtype)

    x = x + jnp.concatenate([y_a, y_b], axis=-1) @ w_out
    h2 = rmsnorm(x, norm_ffn)
    x = x + (jax.nn.silu(h2 @ w_gate) * (h2 @ w_up)) @ w_down
    return x, new_buf, s_new


def setup_inputs(seed: int = 0) -> dict:
    key = jax.random.key(seed)
    ks = jax.random.split(key, 16)
    nrm = lambda k, shape, s: jax.random.normal(k, shape, jnp.float32) * s
    gain = lambda k, shape: 1.0 + nrm(k, shape, 0.01)
    return {
        "x_prompt": nrm(ks[0], (BATCH, SEQ, D_MODEL), 1.0),
        "x_sample": nrm(ks[1], (DEC_BATCH, DEC_SEQ, D_MODEL), 1.0),
        "cache_conv": nrm(ks[2], (DEPTH, DEC_BATCH, CONV_W - 1, D_CONV), 1.0),
        "state_hgrn": nrm(ks[3], (DEPTH, DEC_BATCH, HG_HEADS, HG_DK, HG_DV), 0.5),
        "norm_mix": gain(ks[4], (DEPTH, D_MODEL)),
        "w_in": nrm(ks[5], (DEPTH, D_MODEL, N_IN), D_MODEL ** -0.5),
        "conv_w": nrm(ks[6], (DEPTH, CONV_W, D_CONV), CONV_W ** -0.5),
        "lb_logits": nrm(ks[7], (DEPTH + 1, D_HG), 0.1),
        "hg_norm": gain(ks[8], (DEPTH, HG_DV)),
        "w_out": nrm(ks[9], (DEPTH, D_MIX, D_MODEL), D_MIX ** -0.5),
        "norm_ffn": gain(ks[10], (DEPTH, D_MODEL)),
        "w_gate": nrm(ks[11], (DEPTH, D_MODEL, D_FF), D_MODEL ** -0.5),
        "w_up": nrm(ks[12], (DEPTH, D_MODEL, D_FF), D_MODEL ** -0.5),
        "w_down": nrm(ks[13], (DEPTH, D_FF, D_MODEL), D_FF ** -0.5),
        "norm_final": gain(ks[14], (D_MODEL,)),
    }


def reference(x_prompt, x_sample, cache_conv, state_hgrn, norm_mix, w_in, conv_w,
              lb_logits, hg_norm, w_out, norm_ffn, w_gate, w_up, w_down, norm_final):
    lb_all = jnp.cumsum(jax.nn.softmax(lb_logits.astype(jnp.float32), axis=0), axis=0)
    xp, xs = x_prompt, x_sample
    bp = x_prompt.shape[0]
    conv_p, hg_p, conv_s, hg_s = [], [], [], []
    for l in range(DEPTH):
        params = (norm_mix[l], w_in[l], conv_w[l], hg_norm[l], w_out[l],
                  norm_ffn[l], w_gate[l], w_up[l], w_down[l])
        buf0 = jnp.zeros((bp, CONV_W - 1, D_CONV), xp.dtype)
        s0 = jnp.zeros((bp, HG_HEADS, HG_DK, HG_DV), jnp.float32)
        xp, cb_p, sp = layer(xp, buf0, s0, lb_all[l], *params)
        xs, cb_s, ss = layer(xs, cache_conv[l], state_hgrn[l], lb_all[l], *params)
        conv_p.append(cb_p)
        hg_p.append(sp.astype(x_prompt.dtype))
        conv_s.append(cb_s.astype(cache_conv.dtype))
        hg_s.append(ss.astype(state_hgrn.dtype))
    y_prompt = rmsnorm(xp, norm_final)
    y_sample = rmsnorm(xs, norm_final)
    return (y_prompt, y_sample, jnp.stack(conv_p), jnp.stack(hg_p),
            jnp.stack(conv_s), jnp.stack(hg_s))
```

```python
import functools

import jax
import jax.numpy as jnp
from jax import lax
from jax.experimental import pallas as pl
from jax.experimental.pallas import tpu as pltpu

LANE = 128
BF16_ROWS = 16
SCAN_BLOCK = 16
HEAD_DIM = 128
CONV_W = 3
NORM_EPS = 1e-6
V7X_VMEM_BYTES = 64 * 1024 * 1024
VMEM_LIMIT = V7X_VMEM_BYTES - 6 * 1024 * 1024

F32 = jnp.float32
BF16 = jnp.bfloat16


def _largest_tile(n, cap, quantum):
    best = None
    for t in range(quantum, min(n, cap) + 1, quantum):
        if n % t == 0:
            best = t
    if best is None:
        raise ValueError(f"no tile for n={n} cap={cap} quantum={quantum}")
    return best


def _round_up(n, m):
    return (n + m - 1) // m * m


def _params(sem):
    return pltpu.CompilerParams(dimension_semantics=sem, vmem_limit_bytes=VMEM_LIMIT)


def _rms_rows(x, w):
    ms = jnp.mean(x * x, axis=-1, keepdims=True)
    return x * lax.rsqrt(ms + NORM_EPS) * w


def _silu(x):
    return x * jax.nn.sigmoid(x)


def _norm_two_src_kernel(xp_ref, xs_ref, w_ref, o_ref, *, n_p):
    i = pl.program_id(0)

    @pl.when(i < n_p)
    def _():
        o_ref[...] = _rms_rows(xp_ref[...], w_ref[...]).astype(o_ref.dtype)

    @pl.when(i >= n_p)
    def _():
        o_ref[...] = _rms_rows(xs_ref[...], w_ref[...]).astype(o_ref.dtype)


def _norm_two_src(xp, xs, w, tr):
    mp, d = xp.shape
    ms = xs.shape[0]
    n_p, n_s = mp // tr, ms // tr
    return pl.pallas_call(
        functools.partial(_norm_two_src_kernel, n_p=n_p),
        grid=(n_p + n_s,),
        in_specs=[
            pl.BlockSpec((tr, d), lambda i: (jnp.minimum(i, n_p - 1), 0)),
            pl.BlockSpec((tr, d), lambda i: (jnp.maximum(i - n_p, 0), 0)),
            pl.BlockSpec((1, d), lambda i: (0, 0)),
        ],
        out_specs=pl.BlockSpec((tr, d), lambda i: (i, 0)),
        out_shape=jax.ShapeDtypeStruct((mp + ms, d), BF16),
        compiler_params=_params(("parallel",)),
        name="norm1",
    )(xp, xs, w)


def _norm_kernel(x_ref, w_ref, o_ref):
    o_ref[...] = _rms_rows(x_ref[...], w_ref[...]).astype(o_ref.dtype)


def _norm(x, w, tr):
    m, d = x.shape
    return pl.pallas_call(
        _norm_kernel,
        grid=(m // tr,),
        in_specs=[pl.BlockSpec((tr, d), lambda i: (i, 0)), pl.BlockSpec((1, d), lambda i: (0, 0))],
        out_specs=pl.BlockSpec((tr, d), lambda i: (i, 0)),
        out_shape=jax.ShapeDtypeStruct((m, d), BF16),
        compiler_params=_params(("parallel",)),
        name="norm2",
    )(x, w)


def _norm_two_sink_kernel(x_ref, w_ref, op_ref, os_ref, *, n_p):
    i = pl.program_id(0)

    @pl.when(i < n_p)
    def _():
        op_ref[...] = _rms_rows(x_ref[...], w_ref[...])

    @pl.when(i >= n_p)
    def _():
        os_ref[...] = _rms_rows(x_ref[...], w_ref[...])


def _norm_two_sink(x, w, mp, tr):
    m, d = x.shape
    ms = m - mp
    n_p, n_s = mp // tr, ms // tr
    return pl.pallas_call(
        functools.partial(_norm_two_sink_kernel, n_p=n_p),
        grid=(n_p + n_s,),
        in_specs=[pl.BlockSpec((tr, d), lambda i: (i, 0)), pl.BlockSpec((1, d), lambda i: (0, 0))],
        out_specs=[
            pl.BlockSpec((tr, d), lambda i: (jnp.minimum(i, n_p - 1), 0)),
            pl.BlockSpec((tr, d), lambda i: (jnp.maximum(i - n_p, 0), 0)),
        ],
        out_shape=[jax.ShapeDtypeStruct((mp, d), F32), jax.ShapeDtypeStruct((ms, d), F32)],
        compiler_params=_params(("arbitrary",)),
        name="final_norm",
    )(x, w)


def _in_proj_kernel(a_ref, b_ref, o_ref):
    acc = jnp.dot(a_ref[...], b_ref[...], preferred_element_type=F32)
    for g in range(o_ref.shape[0]):
        o_ref[g] = acc[:, g * LANE:(g + 1) * LANE]


def _in_proj(h, w, tm, tn):
    m, k = h.shape
    n = w.shape[1]
    return pl.pallas_call(
        _in_proj_kernel,
        grid=(m // tm, n // tn),
        in_specs=[pl.BlockSpec((tm, k), lambda i, j: (i, 0)), pl.BlockSpec((k, tn), lambda i, j: (0, j))],
        out_specs=pl.BlockSpec((tn // LANE, tm, LANE), lambda i, j: (j, i, 0)),
        out_shape=jax.ShapeDtypeStruct((n // LANE, m, LANE), F32),
        compiler_params=_params(("parallel", "parallel")),
        name="in_proj",
    )(h, w)


def _mixer_kernel(*refs, tc, n_chunks, has_cache, layer):
    if has_cache:
        (bg_ref, cg_ref, u_ref, q_ref, fz_ref, iv_ref, og_ref, convw_ref, lbl_ref, hgn_ref,
         cbuf_ref, s0_ref, _ya_in, _yb_in, ya_ref, yb_ref, nconv_ref, nst_ref, st_ref, carry_ref) = refs
    else:
        (bg_ref, cg_ref, u_ref, q_ref, fz_ref, iv_ref, og_ref, convw_ref, lbl_ref, hgn_ref,
         ya_ref, yb_ref, nconv_ref, nst_ref, st_ref, carry_ref) = refs
    c = pl.program_id(2)
    nsb = tc // SCAN_BLOCK

    @pl.when(c == 0)
    def _init():
        if has_cache:
            st_ref[...] = s0_ref[0, 0].T
            carry_ref[...] = cbuf_ref[0]
        else:
            st_ref[...] = jnp.zeros_like(st_ref)
            carry_ref[...] = jnp.zeros_like(carry_ref)

    row = lax.broadcasted_iota(jnp.int32, (tc, LANE), 0)

    cu = cg_ref[0] * u_ref[0]
    prev2 = carry_ref[0:1, :]
    prev1 = carry_ref[1:2, :]
    s1 = jnp.where(row >= 1, pltpu.roll(cu, 1, 0), prev1)
    s2 = jnp.where(row >= 2, pltpu.roll(cu, 2, 0), jnp.where(row == 1, prev1, prev2))
    conv = convw_ref[0:1, :] * s2 + convw_ref[1:2, :] * s1 + convw_ref[2:3, :] * cu
    ya_ref[...] = (bg_ref[0] * conv).astype(ya_ref.dtype)
    tail = cu[tc - (CONV_W - 1):tc, :]
    carry_ref[...] = tail
    nconv_ref[0] = tail

    logits = lbl_ref[...]
    ex = jnp.exp(logits - jnp.max(logits, axis=0, keepdims=True))
    lb = jnp.sum(ex[0:layer + 1, :], axis=0, keepdims=True) / jnp.sum(ex, axis=0, keepdims=True)
    z = fz_ref[0]
    f = lb + (1.0 - lb) * jax.nn.sigmoid(z)
    g_log = jnp.log(f)
    kk = (1.0 - lb) * jax.nn.sigmoid(-z)
    qf = _silu(q_ref[0])
    vb = iv_ref[0].astype(BF16)

    r_in = row & (SCAN_BLOCK - 1)
    lc = g_log
    sh = 1
    while sh < SCAN_BLOCK:
        lc = lc + jnp.where(r_in >= sh, pltpu.roll(lc, sh, 0), 0.0)
        sh *= 2

    tri = (lax.broadcasted_iota(jnp.int32, (SCAN_BLOCK, SCAN_BLOCK), 0)
           >= lax.broadcasted_iota(jnp.int32, (SCAN_BLOCK, SCAN_BLOCK), 1))
    mid = SCAN_BLOCK // 2
    nt = (((1,), (1,)), ((), ()))
    tn = (((0,), (0,)), ((), ()))
    st = st_ref[...]
    outs = []
    for j in range(nsb):
        lo = j * SCAN_BLOCK
        lcj = lc[lo:lo + SCAN_BLOCK, :]
        ref = lc[lo + mid - 1:lo + mid, :]
        llast = lc[lo + SCAN_BLOCK - 1:lo + SCAN_BLOCK, :]
        qj = qf[lo:lo + SCAN_BLOCK, :]
        kj = kk[lo:lo + SCAN_BLOCK, :]
        vj = vb[lo:lo + SCAN_BLOCK, :]
        qd = (qj * jnp.exp(lcj - ref)).astype(BF16)
        kd = (kj * jnp.exp(ref - lcj)).astype(BF16)
        att = lax.dot_general(qd, kd, nt, preferred_element_type=F32)
        att = jnp.where(tri, att, 0.0).astype(BF16)
        o = jnp.dot(att, vj, preferred_element_type=F32)
        qe = (qj * jnp.exp(lcj)).astype(BF16)
        o = o + lax.dot_general(qe, st.astype(BF16), nt, preferred_element_type=F32)
        kdec = (kj * jnp.exp(llast - lcj)).astype(BF16)
        upd = lax.dot_general(vj, kdec, tn, preferred_element_type=F32)
        st = st * jnp.exp(llast) + upd
        outs.append(o)
    st_ref[...] = st
    o_all = jnp.concatenate(outs, axis=0) if nsb > 1 else outs[0]
    ms = jnp.mean(o_all * o_all, axis=-1, keepdims=True)
    yb = o_all * lax.rsqrt(ms + NORM_EPS) * hgn_ref[...] * _silu(og_ref[0])
    yb_ref[...] = yb.astype(yb_ref.dtype)

    @pl.when(c == n_chunks - 1)
    def _fin():
        nst_ref[0, 0] = st.T


def _mixer(proj, conv_w, lb_logits, hg_norm, *, row0, nseq, t, tc, n_groups, m_all, layer,
           cache=None, state=None, ya_in=None, yb_in=None):
    has_cache = cache is not None
    n_chunks = t // tc
    rb0 = row0 // tc

    def pspec(seg):
        return pl.BlockSpec((1, tc, LANE), lambda b, h, c: (seg * n_groups + h, rb0 + b * n_chunks + c, 0))

    in_specs = [pspec(s) for s in range(7)]
    in_specs += [
        pl.BlockSpec((CONV_W, LANE), lambda b, h, c: (0, h)),
        pl.BlockSpec((lb_logits.shape[0], LANE), lambda b, h, c: (0, h)),
        pl.BlockSpec((1, LANE), lambda b, h, c: (0, 0)),
    ]
    args = [proj] * 7 + [conv_w, lb_logits, hg_norm]
    io_alias = {}
    if has_cache:
        in_specs += [
            pl.BlockSpec((1, CONV_W - 1, LANE), lambda b, h, c: (b, 0, h)),
            pl.BlockSpec((1, 1, HEAD_DIM, HEAD_DIM), lambda b, h, c: (b, h, 0, 0)),
        ]
        args += [cache, state]
    if ya_in is not None:
        in_specs += [pl.BlockSpec(memory_space=pl.ANY), pl.BlockSpec(memory_space=pl.ANY)]
        io_alias = {len(args): 0, len(args) + 1: 1}
        args += [ya_in, yb_in]
    d_half = n_groups * LANE
    yspec = pl.BlockSpec((tc, LANE), lambda b, h, c: (rb0 + b * n_chunks + c, h))
    out_specs = [
        yspec, yspec,
        pl.BlockSpec((1, CONV_W - 1, LANE), lambda b, h, c: (b, 0, h)),
        pl.BlockSpec((1, 1, HEAD_DIM, HEAD_DIM), lambda b, h, c: (b, h, 0, 0)),
    ]
    out_shape = [
        jax.ShapeDtypeStruct((m_all, d_half), BF16),
        jax.ShapeDtypeStruct((m_all, d_half), BF16),
        jax.ShapeDtypeStruct((nseq, CONV_W - 1, d_half), F32),
        jax.ShapeDtypeStruct((nseq, n_groups, HEAD_DIM, HEAD_DIM), F32),
    ]
    kern = functools.partial(_mixer_kernel, tc=tc, n_chunks=n_chunks,
                             has_cache=has_cache, layer=layer)
    if has_cache != (ya_in is not None):
        raise ValueError("the cached call is the one that extends the prompt call's outputs")
    return pl.pallas_call(
        kern,
        grid=(nseq, n_groups, n_chunks),
        in_specs=in_specs,
        out_specs=out_specs,
        out_shape=out_shape,
        scratch_shapes=[pltpu.VMEM((HEAD_DIM, HEAD_DIM), F32), pltpu.VMEM((CONV_W - 1, LANE), F32)],
        input_output_aliases=io_alias,
        compiler_params=_params(("parallel", "parallel", "arbitrary")),
        name="mixer_cached" if has_cache else "mixer_prompt",
    )(*args)


def _out_proj_kernel(ya_ref, yb_ref, wa_ref, wb_ref, xp_ref, xs_ref, o_ref, *, n_p):
    i = pl.program_id(0)
    acc = jnp.dot(ya_ref[...], wa_ref[...], preferred_element_type=F32)
    acc = acc + jnp.dot(yb_ref[...], wb_ref[...], preferred_element_type=F32)

    @pl.when(i < n_p)
    def _():
        o_ref[...] = xp_ref[...] + acc

    @pl.when(i >= n_p)
    def _():
        o_ref[...] = xs_ref[...] + acc


def _out_proj(ya, yb, w, xp, xs, tm, tn):
    m, dh = ya.shape
    n = w.shape[1]
    n_p = xp.shape[0] // tm
    return pl.pallas_call(
        functools.partial(_out_proj_kernel, n_p=n_p),
        grid=(m // tm, n // tn),
        in_specs=[
            pl.BlockSpec((tm, dh), lambda i, j: (i, 0)),
            pl.BlockSpec((tm, dh), lambda i, j: (i, 0)),
            pl.BlockSpec((dh, tn), lambda i, j: (0, j)),
            pl.BlockSpec((dh, tn), lambda i, j: (1, j)),
            pl.BlockSpec((tm, tn), lambda i, j: (jnp.minimum(i, n_p - 1), j)),
            pl.BlockSpec((tm, tn), lambda i, j: (jnp.maximum(i - n_p, 0), j)),
        ],
        out_specs=pl.BlockSpec((tm, tn), lambda i, j: (i, j)),
        out_shape=jax.ShapeDtypeStruct((m, n), F32),
        compiler_params=_params(("parallel", "parallel")),
        name="out_proj",
    )(ya, yb, w, w, xp, xs)


def _ffn_up_kernel(a_ref, wg_ref, wu_ref, o_ref):
    a = a_ref[...]
    g = jnp.dot(a, wg_ref[...], preferred_element_type=F32)
    u = jnp.dot(a, wu_ref[...], preferred_element_type=F32)
    o_ref[...] = (_silu(g) * u).astype(o_ref.dtype)


def _ffn_up(h, wg, wu, tm, tn):
    m, k = h.shape
    n = wg.shape[1]
    return pl.pallas_call(
        _ffn_up_kernel,
        grid=(m // tm, n // tn),
        in_specs=[
            pl.BlockSpec((tm, k), lambda i, j: (i, 0)),
            pl.BlockSpec((k, tn), lambda i, j: (0, j)),
            pl.BlockSpec((k, tn), lambda i, j: (0, j)),
        ],
        out_specs=pl.BlockSpec((tm, tn), lambda i, j: (i, j)),
        out_shape=jax.ShapeDtypeStruct((m, n), BF16),
        compiler_params=_params(("parallel", "parallel")),
        name="ffn_up",
    )(h, wg, wu)


def _ffn_down_kernel(a_ref, w_ref, x_ref, o_ref):
    k = pl.program_id(2)
    part = jnp.dot(a_ref[...], w_ref[...], preferred_element_type=F32)

    @pl.when(k == 0)
    def _():
        o_ref[...] = x_ref[...] + part

    @pl.when(k > 0)
    def _():
        o_ref[...] += part


def _ffn_down(act, w, x1, tm, tn, tk):
    m, kf = act.shape
    n = w.shape[1]
    return pl.pallas_call(
        _ffn_down_kernel,
        grid=(m // tm, n // tn, kf // tk),
        in_specs=[
            pl.BlockSpec((tm, tk), lambda i, j, k: (i, k)),
            pl.BlockSpec((tk, tn), lambda i, j, k: (k, j)),
            pl.BlockSpec((tm, tn), lambda i, j, k: (i, j)),
        ],
        out_specs=pl.BlockSpec((tm, tn), lambda i, j, k: (i, j)),
        out_shape=jax.ShapeDtypeStruct((m, n), F32),
        compiler_params=_params(("parallel", "parallel", "arbitrary")),
        name="ffn_down",
    )(act, w, x1)


def _tiles(mp, ms, d, n_in, f_pad):
    m = mp + ms
    tr = _largest_tile(_gcd(mp, ms), 512, BF16_ROWS)
    return dict(
        tr=tr,
        tm=_largest_tile(m, 1056, BF16_ROWS),
        tn_in=_largest_tile(n_in, 1024, LANE),
        tn_out=_largest_tile(d, 1024, LANE),
        tn_up=_largest_tile(f_pad, 512, LANE),
        tn_down=_largest_tile(d, 1024, LANE),
        tk_down=_largest_tile(f_pad, 2816, LANE),
    )


def _gcd(a, b):
    while b:
        a, b = b, a % b
    return a


def kernel(x_prompt, x_sample, cache_conv, state_hgrn, norm_mix, w_in, conv_w, lb_logits, hg_norm,
           w_out, norm_ffn, w_gate, w_up, w_down, norm_final):
    bp, tp, d = x_prompt.shape
    bs, ts, _ = x_sample.shape
    depth = w_in.shape[0]
    if depth != 1:
        raise NotImplementedError("one layer")
    layer = 0
    n_in = w_in.shape[2]
    d_ff = w_gate.shape[2]
    d_half = conv_w.shape[2]
    n_groups = d_half // LANE
    if n_in != 7 * d_half or state_hgrn.shape[2] != n_groups or w_out.shape[1] != 2 * d_half:
        raise ValueError("unexpected layer geometry")
    mp, ms = bp * tp, bs * ts
    m = mp + ms
    f_pad = _round_up(d_ff, 1024)
    t = _tiles(mp, ms, d, n_in, f_pad)

    xp = x_prompt.reshape(mp, d)
    xs = x_sample.reshape(ms, d)
    w_in_b = w_in[layer].astype(BF16)
    w_out_b = w_out[layer].astype(BF16)
    pad_f = f_pad - d_ff
    w_gate_b = jnp.pad(w_gate[layer].astype(BF16), ((0, 0), (0, pad_f)))
    w_up_b = jnp.pad(w_up[layer].astype(BF16), ((0, 0), (0, pad_f)))
    w_down_b = jnp.pad(w_down[layer].astype(BF16), ((0, pad_f), (0, 0)))

    h = _norm_two_src(xp, xs, norm_mix[layer].reshape(1, d), t["tr"])
    proj = _in_proj(h, w_in_b, t["tm"], t["tn_in"])

    tc_p = _largest_tile(tp, 256, SCAN_BLOCK)
    tc_s = _largest_tile(ts, 256, SCAN_BLOCK)
    mix_args = dict(n_groups=n_groups, m_all=m, layer=layer)
    hgn = hg_norm[layer].reshape(1, HEAD_DIM)
    ya, yb, conv_p, hg_p = _mixer(proj, conv_w[layer], lb_logits, hgn, row0=0, nseq=bp, t=tp, tc=tc_p,
                                  **mix_args)
    ya, yb, conv_s, hg_s = _mixer(proj, conv_w[layer], lb_logits, hgn, row0=mp, nseq=bs, t=ts, tc=tc_s,
                                  cache=cache_conv[layer], state=state_hgrn[layer], ya_in=ya, yb_in=yb,
                                  **mix_args)

    x1 = _out_proj(ya, yb, w_out_b, xp, xs, t["tr"], t["tn_out"])
    h2 = _norm(x1, norm_ffn[layer].reshape(1, d), t["tr"])
    act = _ffn_up(h2, w_gate_b, w_up_b, t["tm"], t["tn_up"])
    x2 = _ffn_down(act, w_down_b, x1, t["tm"], t["tn_down"], t["tk_down"])
    y_p, y_s = _norm_two_sink(x2, norm_final.reshape(1, d), mp, t["tr"])

    return (y_p.reshape(bp, tp, d), y_s.reshape(bs, ts, d),
            conv_p[None], hg_p[None].astype(x_prompt.dtype),
            conv_s[None].astype(cache_conv.dtype), hg_s[None].astype(state_hgrn.dtype))
```

```python
import functools

import jax
import jax.numpy as jnp
from jax import lax
from jax.experimental import pallas as pl
from jax.experimental.pallas import tpu as pltpu

LANE = 128
SUBLANES = 8
BF16_ROWS = 16
SCAN_BLOCK = 16
HEAD_DIM = 128
CONV_W = 3
NORM_EPS = 1e-6
V7X_VMEM_BYTES = 64 * 1024 * 1024
VMEM_LIMIT = V7X_VMEM_BYTES - 6 * 1024 * 1024

F32 = jnp.float32
BF16 = jnp.bfloat16


def _largest_tile(n, cap, quantum):
    best = None
    for t in range(quantum, min(n, cap) + 1, quantum):
        if n % t == 0:
            best = t
    if best is None:
        raise ValueError(f"no tile for n={n} cap={cap} quantum={quantum}")
    return best


def _round_up(n, m):
    return (n + m - 1) // m * m


def _params(sem):
    return pltpu.CompilerParams(dimension_semantics=sem, vmem_limit_bytes=VMEM_LIMIT)


def _rms_rows(x, w):
    ms = jnp.mean(x * x, axis=-1, keepdims=True)
    return x * lax.rsqrt(ms + NORM_EPS) * w


def _silu(x):
    return x * jax.nn.sigmoid(x)


def _norm_two_src_kernel(xp_ref, xs_ref, w_ref, o_ref, *, n_p):
    i = pl.program_id(0)

    @pl.when(i < n_p)
    def _():
        o_ref[...] = _rms_rows(xp_ref[...], w_ref[...]).astype(o_ref.dtype)

    @pl.when(i >= n_p)
    def _():
        o_ref[...] = _rms_rows(xs_ref[...], w_ref[...]).astype(o_ref.dtype)


def _norm_two_src(xp, xs, w, tr):
    mp, d = xp.shape
    ms = xs.shape[0]
    n_p, n_s = mp // tr, ms // tr
    return pl.pallas_call(
        functools.partial(_norm_two_src_kernel, n_p=n_p),
        grid=(n_p + n_s,),
        in_specs=[
            pl.BlockSpec((tr, d), lambda i: (jnp.minimum(i, n_p - 1), 0)),
            pl.BlockSpec((tr, d), lambda i: (jnp.maximum(i - n_p, 0), 0)),
            pl.BlockSpec((1, d), lambda i: (0, 0)),
        ],
        out_specs=pl.BlockSpec((tr, d), lambda i: (i, 0)),
        out_shape=jax.ShapeDtypeStruct((mp + ms, d), BF16),
        compiler_params=_params(("parallel",)),
        name="norm1",
    )(xp, xs, w)


def _norm_kernel(x_ref, w_ref, o_ref):
    o_ref[...] = _rms_rows(x_ref[...], w_ref[...]).astype(o_ref.dtype)


def _norm(x, w, tr):
    m, d = x.shape
    return pl.pallas_call(
        _norm_kernel,
        grid=(m // tr,),
        in_specs=[pl.BlockSpec((tr, d), lambda i: (i, 0)), pl.BlockSpec((1, d), lambda i: (0, 0))],
        out_specs=pl.BlockSpec((tr, d), lambda i: (i, 0)),
        out_shape=jax.ShapeDtypeStruct((m, d), BF16),
        compiler_params=_params(("parallel",)),
        name="norm2",
    )(x, w)


def _norm_two_sink_kernel(x_ref, w_ref, op_ref, os_ref, *, n_p):
    i = pl.program_id(0)

    @pl.when(i < n_p)
    def _():
        op_ref[...] = _rms_rows(x_ref[...], w_ref[...])

    @pl.when(i >= n_p)
    def _():
        os_ref[...] = _rms_rows(x_ref[...], w_ref[...])


def _norm_two_sink(x, w, mp, tr):
    m, d = x.shape
    ms = m - mp
    n_p, n_s = mp // tr, ms // tr
    return pl.pallas_call(
        functools.partial(_norm_two_sink_kernel, n_p=n_p),
        grid=(n_p + n_s,),
        in_specs=[pl.BlockSpec((tr, d), lambda i: (i, 0)), pl.BlockSpec((1, d), lambda i: (0, 0))],
        out_specs=[
            pl.BlockSpec((tr, d), lambda i: (jnp.minimum(i, n_p - 1), 0)),
            pl.BlockSpec((tr, d), lambda i: (jnp.maximum(i - n_p, 0), 0)),
        ],
        out_shape=[jax.ShapeDtypeStruct((mp, d), F32), jax.ShapeDtypeStruct((ms, d), F32)],
        compiler_params=_params(("arbitrary",)),
        name="final_norm",
    )(x, w)


def _in_proj_kernel(a_ref, b_ref, o_ref):
    acc = jnp.dot(a_ref[...], b_ref[...].astype(BF16), preferred_element_type=F32)
    for g in range(o_ref.shape[0]):
        o_ref[g] = acc[:, g * LANE:(g + 1) * LANE]


def _in_proj(h, w, tm, tn):
    m, k = h.shape
    n = w.shape[1]
    return pl.pallas_call(
        _in_proj_kernel,
        grid=(m // tm, n // tn),
        in_specs=[pl.BlockSpec((tm, k), lambda i, j: (i, 0)), pl.BlockSpec((k, tn), lambda i, j: (0, j))],
        out_specs=pl.BlockSpec((tn // LANE, tm, LANE), lambda i, j: (j, i, 0)),
        out_shape=jax.ShapeDtypeStruct((n // LANE, m, LANE), F32),
        compiler_params=_params(("parallel", "parallel")),
        name="in_proj",
    )(h, w)


def _mixer_kernel(*refs, tc, n_chunks, has_cache, layer):
    if has_cache:
        (bg_ref, cg_ref, u_ref, q_ref, fz_ref, iv_ref, og_ref, convw_ref, lbl_ref, hgn_ref,
         cbuf_ref, s0_ref, _ya_in, _yb_in, ya_ref, yb_ref, nconv_ref, nst_ref, st_ref, cu_ref) = refs
    else:
        (bg_ref, cg_ref, u_ref, q_ref, fz_ref, iv_ref, og_ref, convw_ref, lbl_ref, hgn_ref,
         ya_ref, yb_ref, nconv_ref, nst_ref, st_ref, cu_ref) = refs
    c = pl.program_id(2)
    nsb = tc // SCAN_BLOCK

    @pl.when(c == 0)
    def _init():
        if has_cache:
            st_ref[...] = s0_ref[0, 0].T
            cu_ref[SUBLANES - (CONV_W - 1):SUBLANES, :] = cbuf_ref[0]
        else:
            st_ref[...] = jnp.zeros_like(st_ref)
            cu_ref[0:SUBLANES, :] = jnp.zeros((SUBLANES, LANE), F32)

    cu = cg_ref[0] * u_ref[0]
    cu_ref[SUBLANES:SUBLANES + tc, :] = cu
    s1 = cu_ref[SUBLANES - 1:SUBLANES - 1 + tc, :]
    s2 = cu_ref[SUBLANES - 2:SUBLANES - 2 + tc, :]
    conv = convw_ref[0:1, :] * s2 + convw_ref[1:2, :] * s1 + convw_ref[2:3, :] * cu
    ya_ref[...] = (bg_ref[0] * conv).astype(ya_ref.dtype)
    tail = cu[tc - (CONV_W - 1):tc, :]
    cu_ref[SUBLANES - (CONV_W - 1):SUBLANES, :] = tail
    nconv_ref[0] = tail

    logits = lbl_ref[...]
    ex = jnp.exp(logits - jnp.max(logits, axis=0, keepdims=True))
    lb = jnp.sum(ex[0:layer + 1, :], axis=0, keepdims=True) / jnp.sum(ex, axis=0, keepdims=True)
    z = fz_ref[0]
    f = lb + (1.0 - lb) * jax.nn.sigmoid(z)
    g_log = jnp.log(f)
    kk = (1.0 - lb) * jax.nn.sigmoid(-z)
    qf = _silu(q_ref[0])
    vb = iv_ref[0].astype(BF16)

    sub = lax.broadcasted_iota(jnp.int32, (SUBLANES, LANE), 0)

    def prefix8(x):
        sh = 1
        while sh < SUBLANES:
            x = x + jnp.where(sub >= sh, pltpu.roll(x, sh, 0), 0.0)
            sh *= 2
        return x

    tri = (lax.broadcasted_iota(jnp.int32, (SCAN_BLOCK, SCAN_BLOCK), 0)
           >= lax.broadcasted_iota(jnp.int32, (SCAN_BLOCK, SCAN_BLOCK), 1))
    mid = SCAN_BLOCK // 2
    nt = (((1,), (1,)), ((), ()))
    tn = (((0,), (0,)), ((), ()))
    o_intra, upds, decays, qes = [], [], [], []
    for j in range(nsb):
        lo = j * SCAN_BLOCK
        parts = []
        total = None
        for p in range(SCAN_BLOCK // SUBLANES):
            part = prefix8(g_log[lo + p * SUBLANES:lo + (p + 1) * SUBLANES, :])
            if total is not None:
                part = part + total
            total = part[SUBLANES - 1:SUBLANES, :]
            parts.append(part)
        lcj = jnp.concatenate(parts, axis=0)
        ref = lcj[mid - 1:mid, :]
        llast = total
        qj = qf[lo:lo + SCAN_BLOCK, :]
        kj = kk[lo:lo + SCAN_BLOCK, :]
        vj = vb[lo:lo + SCAN_BLOCK, :]
        qd = (qj * jnp.exp(lcj - ref)).astype(BF16)
        kd = (kj * jnp.exp(ref - lcj)).astype(BF16)
        att = lax.dot_general(qd, kd, nt, preferred_element_type=F32)
        att = jnp.where(tri, att, 0.0).astype(BF16)
        o_intra.append(jnp.dot(att, vj, preferred_element_type=F32))
        kdec = (kj * jnp.exp(llast - lcj)).astype(BF16)
        upds.append(lax.dot_general(vj, kdec, tn, preferred_element_type=F32))
        decays.append(jnp.exp(llast))
        qes.append((qj * jnp.exp(lcj)).astype(BF16))
    st = st_ref[...]
    states = []
    for j in range(nsb):
        states.append(st.astype(BF16))
        st = st * decays[j] + upds[j]
    st_ref[...] = st
    outs = [o_intra[j] + lax.dot_general(qes[j], states[j], nt, preferred_element_type=F32)
            for j in range(nsb)]
    o_all = jnp.concatenate(outs, axis=0) if nsb > 1 else outs[0]
    ms = jnp.mean(o_all * o_all, axis=-1, keepdims=True)
    yb = o_all * lax.rsqrt(ms + NORM_EPS) * hgn_ref[...] * _silu(og_ref[0])
    yb_ref[...] = yb.astype(yb_ref.dtype)

    @pl.when(c == n_chunks - 1)
    def _fin():
        nst_ref[0, 0] = st.T


def _mixer(proj, conv_w, lb_logits, hg_norm, *, row0, nseq, t, tc, n_groups, m_all, layer,
           cache=None, state=None, ya_in=None, yb_in=None):
    has_cache = cache is not None
    n_chunks = t // tc
    rb0 = row0 // tc

    def pspec(seg):
        return pl.BlockSpec((1, tc, LANE), lambda b, h, c: (seg * n_groups + h, rb0 + b * n_chunks + c, 0))

    in_specs = [pspec(s) for s in range(7)]
    in_specs += [
        pl.BlockSpec((CONV_W, LANE), lambda b, h, c: (0, h)),
        pl.BlockSpec((lb_logits.shape[0], LANE), lambda b, h, c: (0, h)),
        pl.BlockSpec((1, LANE), lambda b, h, c: (0, 0)),
    ]
    args = [proj] * 7 + [conv_w, lb_logits, hg_norm]
    io_alias = {}
    if has_cache:
        in_specs += [
            pl.BlockSpec((1, CONV_W - 1, LANE), lambda b, h, c: (b, 0, h)),
            pl.BlockSpec((1, 1, HEAD_DIM, HEAD_DIM), lambda b, h, c: (b, h, 0, 0)),
        ]
        args += [cache, state]
    if ya_in is not None:
        in_specs += [pl.BlockSpec(memory_space=pl.ANY), pl.BlockSpec(memory_space=pl.ANY)]
        io_alias = {len(args): 0, len(args) + 1: 1}
        args += [ya_in, yb_in]
    d_half = n_groups * LANE
    yspec = pl.BlockSpec((tc, LANE), lambda b, h, c: (rb0 + b * n_chunks + c, h))
    out_specs = [
        yspec, yspec,
        pl.BlockSpec((1, CONV_W - 1, LANE), lambda b, h, c: (b, 0, h)),
        pl.BlockSpec((1, 1, HEAD_DIM, HEAD_DIM), lambda b, h, c: (b, h, 0, 0)),
    ]
    out_shape = [
        jax.ShapeDtypeStruct((m_all, d_half), BF16),
        jax.ShapeDtypeStruct((m_all, d_half), BF16),
        jax.ShapeDtypeStruct((nseq, CONV_W - 1, d_half), F32),
        jax.ShapeDtypeStruct((nseq, n_groups, HEAD_DIM, HEAD_DIM), F32),
    ]
    kern = functools.partial(_mixer_kernel, tc=tc, n_chunks=n_chunks,
                             has_cache=has_cache, layer=layer)
    if has_cache != (ya_in is not None):
        raise ValueError("the cached call is the one that extends the prompt call's outputs")
    return pl.pallas_call(
        kern,
        grid=(nseq, n_groups, n_chunks),
        in_specs=in_specs,
        out_specs=out_specs,
        out_shape=out_shape,
        scratch_shapes=[pltpu.VMEM((HEAD_DIM, HEAD_DIM), F32), pltpu.VMEM((SUBLANES + tc, LANE), F32)],
        input_output_aliases=io_alias,
        compiler_params=_params(("parallel", "parallel", "arbitrary")),
        name="mixer_cached" if has_cache else "mixer_prompt",
    )(*args)


def _out_proj_kernel(ya_ref, yb_ref, wa_ref, wb_ref, xp_ref, xs_ref, o_ref, *, n_p):
    i = pl.program_id(0)
    acc = jnp.dot(ya_ref[...], wa_ref[...], preferred_element_type=F32)
    acc = acc + jnp.dot(yb_ref[...], wb_ref[...], preferred_element_type=F32)

    @pl.when(i < n_p)
    def _():
        o_ref[...] = xp_ref[...] + acc

    @pl.when(i >= n_p)
    def _():
        o_ref[...] = xs_ref[...] + acc


def _out_proj(ya, yb, w, xp, xs, tm, tn):
    m, dh = ya.shape
    n = w.shape[1]
    n_p = xp.shape[0] // tm
    return pl.pallas_call(
        functools.partial(_out_proj_kernel, n_p=n_p),
        grid=(m // tm, n // tn),
        in_specs=[
            pl.BlockSpec((tm, dh), lambda i, j: (i, 0)),
            pl.BlockSpec((tm, dh), lambda i, j: (i, 0)),
            pl.BlockSpec((dh, tn), lambda i, j: (0, j)),
            pl.BlockSpec((dh, tn), lambda i, j: (1, j)),
            pl.BlockSpec((tm, tn), lambda i, j: (jnp.minimum(i, n_p - 1), j)),
            pl.BlockSpec((tm, tn), lambda i, j: (jnp.maximum(i - n_p, 0), j)),
        ],
        out_specs=pl.BlockSpec((tm, tn), lambda i, j: (i, j)),
        out_shape=jax.ShapeDtypeStruct((m, n), F32),
        compiler_params=_params(("parallel", "parallel")),
        name="out_proj",
    )(ya, yb, w, w, xp, xs)


def _ffn_up_kernel(a_ref, wg_ref, wu_ref, o_ref, *, n_valid):
    j = pl.program_id(1)

    @pl.when(j < n_valid)
    def _():
        a = a_ref[...]
        g = jnp.dot(a, wg_ref[...].astype(BF16), preferred_element_type=F32)
        u = jnp.dot(a, wu_ref[...].astype(BF16), preferred_element_type=F32)
        o_ref[...] = (_silu(g) * u).astype(o_ref.dtype)

    @pl.when(j >= n_valid)
    def _():
        o_ref[...] = jnp.zeros_like(o_ref)


def _ffn_up(h, wg, wu, tm, tn, f_pad):
    m, k = h.shape
    n_valid = wg.shape[1] // tn
    wspec = pl.BlockSpec((k, tn), lambda i, j: (0, jnp.minimum(j, n_valid - 1)))
    return pl.pallas_call(
        functools.partial(_ffn_up_kernel, n_valid=n_valid),
        grid=(m // tm, f_pad // tn),
        in_specs=[pl.BlockSpec((tm, k), lambda i, j: (i, 0)), wspec, wspec],
        out_specs=pl.BlockSpec((tm, tn), lambda i, j: (i, j)),
        out_shape=jax.ShapeDtypeStruct((m, f_pad), BF16),
        compiler_params=_params(("parallel", "parallel")),
        name="ffn_up",
    )(h, wg, wu)


def _ffn_down_kernel(a_ref, w_ref, x_ref, o_ref, *, k_valid):
    k = pl.program_id(2)
    tk = w_ref.shape[0]
    rows = lax.broadcasted_iota(jnp.int32, w_ref.shape, 0) + k * tk
    w = jnp.where(rows < k_valid, w_ref[...], jnp.zeros_like(w_ref))
    part = jnp.dot(a_ref[...], w, preferred_element_type=F32)

    @pl.when(k == 0)
    def _():
        o_ref[...] = x_ref[...] + part

    @pl.when(k > 0)
    def _():
        o_ref[...] += part


def _ffn_down(act, w, x1, tm, tn, tk):
    m, kf = act.shape
    n = w.shape[1]
    return pl.pallas_call(
        functools.partial(_ffn_down_kernel, k_valid=w.shape[0]),
        grid=(m // tm, n // tn, kf // tk),
        in_specs=[
            pl.BlockSpec((tm, tk), lambda i, j, k: (i, k)),
            pl.BlockSpec((tk, tn), lambda i, j, k: (k, j)),
            pl.BlockSpec((tm, tn), lambda i, j, k: (i, j)),
        ],
        out_specs=pl.BlockSpec((tm, tn), lambda i, j, k: (i, j)),
        out_shape=jax.ShapeDtypeStruct((m, n), F32),
        compiler_params=_params(("parallel", "parallel", "arbitrary")),
        name="ffn_down",
    )(act, w, x1)


def _tiles(mp, ms, d, n_in, d_ff, f_pad):
    m = mp + ms
    tr = _largest_tile(_gcd(mp, ms), 512, BF16_ROWS)
    return dict(
        tr=tr,
        tm=_largest_tile(m, 1056, BF16_ROWS),
        tn_in=_largest_tile(n_in, 512, LANE),
        tn_out=_largest_tile(d, 1024, LANE),
        tn_up=_largest_tile(_gcd(d_ff, f_pad), 256, LANE),
        tn_down=_largest_tile(d, 1024, LANE),
        tk_down=_largest_tile(f_pad, 2816, LANE),
    )


def _gcd(a, b):
    while b:
        a, b = b, a % b
    return a


def kernel(x_prompt, x_sample, cache_conv, state_hgrn, norm_mix, w_in, conv_w, lb_logits, hg_norm,
           w_out, norm_ffn, w_gate, w_up, w_down, norm_final):
    bp, tp, d = x_prompt.shape
    bs, ts, _ = x_sample.shape
    depth = w_in.shape[0]
    if depth != 1:
        raise NotImplementedError("one layer")
    layer = 0
    n_in = w_in.shape[2]
    d_ff = w_gate.shape[2]
    d_half = conv_w.shape[2]
    n_groups = d_half // LANE
    if n_in != 7 * d_half or state_hgrn.shape[2] != n_groups or w_out.shape[1] != 2 * d_half:
        raise ValueError("unexpected layer geometry")
    mp, ms = bp * tp, bs * ts
    m = mp + ms
    f_pad = _round_up(d_ff, 1024)
    t = _tiles(mp, ms, d, n_in, d_ff, f_pad)

    xp = x_prompt.reshape(mp, d)
    xs = x_sample.reshape(ms, d)
    w_out_b = w_out[layer].astype(BF16)
    w_down_b = w_down[layer].astype(BF16)

    h = _norm_two_src(xp, xs, norm_mix[layer].reshape(1, d), t["tr"])
    proj = _in_proj(h, w_in[layer], t["tm"], t["tn_in"])

    tc_p = _largest_tile(tp, 512, SCAN_BLOCK)
    tc_s = _largest_tile(ts, 256, SCAN_BLOCK)
    mix_args = dict(n_groups=n_groups, m_all=m, layer=layer)
    hgn = hg_norm[layer].reshape(1, HEAD_DIM)
    ya, yb, conv_p, hg_p = _mixer(proj, conv_w[layer], lb_logits, hgn, row0=0, nseq=bp, t=tp, tc=tc_p,
                                  **mix_args)
    ya, yb, conv_s, hg_s = _mixer(proj, conv_w[layer], lb_logits, hgn, row0=mp, nseq=bs, t=ts, tc=tc_s,
                                  cache=cache_conv[layer], state=state_hgrn[layer], ya_in=ya, yb_in=yb,
                                  **mix_args)

    x1 = _out_proj(ya, yb, w_out_b, xp, xs, t["tr"], t["tn_out"])
    h2 = _norm(x1, norm_ffn[layer].reshape(1, d), t["tr"])
    act = _ffn_up(h2, w_gate[layer], w_up[layer], t["tm"], t["tn_up"], f_pad)
    x2 = _ffn_down(act, w_down_b, x1, t["tm"], t["tn_down"], t["tk_down"])
    y_p, y_s = _norm_two_sink(x2, norm_final.reshape(1, d), mp, t["tr"])

    return (y_p.reshape(bp, tp, d), y_s.reshape(bs, ts, d),
            conv_p[None], hg_p[None].astype(x_prompt.dtype),
            conv_s[None].astype(cache_conv.dtype), hg_s[None].astype(state_hgrn.dtype))
```

```python
import functools

import jax
import jax.numpy as jnp
from jax import lax
from jax.experimental import pallas as pl
from jax.experimental.pallas import tpu as pltpu

LANE = 128
SUBLANES = 8
BF16_ROWS = 16
SCAN_BLOCK = 16
HEAD_DIM = 128
CONV_W = 3
NORM_EPS = 1e-6
V7X_VMEM_BYTES = 64 * 1024 * 1024
VMEM_LIMIT = V7X_VMEM_BYTES - 6 * 1024 * 1024

F32 = jnp.float32
BF16 = jnp.bfloat16


def _largest_tile(n, cap, quantum):
    best = None
    for t in range(quantum, min(n, cap) + 1, quantum):
        if n % t == 0:
            best = t
    if best is None:
        raise ValueError(f"no tile for n={n} cap={cap} quantum={quantum}")
    return best


def _round_up(n, m):
    return (n + m - 1) // m * m


def _params(sem):
    return pltpu.CompilerParams(dimension_semantics=sem, vmem_limit_bytes=VMEM_LIMIT)


def _rms_rows(x, w):
    ms = jnp.mean(x * x, axis=-1, keepdims=True)
    return x * lax.rsqrt(ms + NORM_EPS) * w


def _silu(x):
    return x * jax.nn.sigmoid(x)


def _norm_two_src_kernel(xp_ref, xs_ref, w_ref, o_ref, *, n_p):
    i = pl.program_id(0)

    @pl.when(i < n_p)
    def _():
        o_ref[...] = _rms_rows(xp_ref[...], w_ref[...]).astype(o_ref.dtype)

    @pl.when(i >= n_p)
    def _():
        o_ref[...] = _rms_rows(xs_ref[...], w_ref[...]).astype(o_ref.dtype)


def _norm_two_src(xp, xs, w, tr):
    mp, d = xp.shape
    ms = xs.shape[0]
    n_p, n_s = mp // tr, ms // tr
    return pl.pallas_call(
        functools.partial(_norm_two_src_kernel, n_p=n_p),
        grid=(n_p + n_s,),
        in_specs=[
            pl.BlockSpec((tr, d), lambda i: (jnp.minimum(i, n_p - 1), 0)),
            pl.BlockSpec((tr, d), lambda i: (jnp.maximum(i - n_p, 0), 0)),
            pl.BlockSpec((1, d), lambda i: (0, 0)),
        ],
        out_specs=pl.BlockSpec((tr, d), lambda i: (i, 0)),
        out_shape=jax.ShapeDtypeStruct((mp + ms, d), BF16),
        compiler_params=_params(("parallel",)),
        name="norm1",
    )(xp, xs, w)


def _norm_kernel(x_ref, w_ref, o_ref):
    o_ref[...] = _rms_rows(x_ref[...], w_ref[...]).astype(o_ref.dtype)


def _norm(x, w, tr):
    m, d = x.shape
    return pl.pallas_call(
        _norm_kernel,
        grid=(m // tr,),
        in_specs=[pl.BlockSpec((tr, d), lambda i: (i, 0)), pl.BlockSpec((1, d), lambda i: (0, 0))],
        out_specs=pl.BlockSpec((tr, d), lambda i: (i, 0)),
        out_shape=jax.ShapeDtypeStruct((m, d), BF16),
        compiler_params=_params(("parallel",)),
        name="norm2",
    )(x, w)


def _norm_two_sink_kernel(x_ref, w_ref, op_ref, os_ref, *, n_p):
    i = pl.program_id(0)

    @pl.when(i < n_p)
    def _():
        op_ref[...] = _rms_rows(x_ref[...], w_ref[...])

    @pl.when(i >= n_p)
    def _():
        os_ref[...] = _rms_rows(x_ref[...], w_ref[...])


def _norm_two_sink(x, w, mp, tr):
    m, d = x.shape
    ms = m - mp
    n_p, n_s = mp // tr, ms // tr
    return pl.pallas_call(
        functools.partial(_norm_two_sink_kernel, n_p=n_p),
        grid=(n_p + n_s,),
        in_specs=[pl.BlockSpec((tr, d), lambda i: (i, 0)), pl.BlockSpec((1, d), lambda i: (0, 0))],
        out_specs=[
            pl.BlockSpec((tr, d), lambda i: (jnp.minimum(i, n_p - 1), 0)),
            pl.BlockSpec((tr, d), lambda i: (jnp.maximum(i - n_p, 0), 0)),
        ],
        out_shape=[jax.ShapeDtypeStruct((mp, d), F32), jax.ShapeDtypeStruct((ms, d), F32)],
        compiler_params=_params(("arbitrary",)),
        name="final_norm",
    )(x, w)


def _in_proj_kernel(a_ref, b_ref, o_ref):
    acc = jnp.dot(a_ref[...], b_ref[...].astype(BF16), preferred_element_type=F32)
    for g in range(o_ref.shape[0]):
        o_ref[g] = acc[:, g * LANE:(g + 1) * LANE]


def _in_proj(h, w, tm, tn):
    m, k = h.shape
    n = w.shape[1]
    return pl.pallas_call(
        _in_proj_kernel,
        grid=(m // tm, n // tn),
        in_specs=[pl.BlockSpec((tm, k), lambda i, j: (i, 0)), pl.BlockSpec((k, tn), lambda i, j: (0, j))],
        out_specs=pl.BlockSpec((tn // LANE, tm, LANE), lambda i, j: (j, i, 0)),
        out_shape=jax.ShapeDtypeStruct((n // LANE, m, LANE), F32),
        compiler_params=_params(("parallel", "parallel")),
        name="in_proj",
    )(h, w)


STATE_LAG = 4
MIXER_BLOCKS_PER_STEP = 128


def _mixer_kernel(*refs, tc, hb, n_chunks, has_cache, layer):
    if has_cache:
        (bg_ref, cg_ref, u_ref, q_ref, fz_ref, iv_ref, og_ref, convw_ref, lbl_ref, hgn_ref,
         cbuf_ref, s0_ref, ya_ref, yb_ref, nconv_ref, nst_ref, st_ref, cu_ref) = refs
    else:
        (bg_ref, cg_ref, u_ref, q_ref, fz_ref, iv_ref, og_ref, convw_ref, lbl_ref, hgn_ref,
         ya_ref, yb_ref, nconv_ref, nst_ref, st_ref, cu_ref) = refs
    c = pl.program_id(2)
    nsb = tc // SCAN_BLOCK

    @pl.when(c == 0)
    def _init():
        for hh in range(hb):
            if has_cache:
                st_ref[hh] = s0_ref[0, hh].T
                cu_ref[hh, SUBLANES - (CONV_W - 1):SUBLANES, :] = cbuf_ref[0, :, hh * LANE:(hh + 1) * LANE]
            else:
                st_ref[hh] = jnp.zeros((HEAD_DIM, HEAD_DIM), F32)
                cu_ref[hh, 0:SUBLANES, :] = jnp.zeros((SUBLANES, LANE), F32)

    hgn = hgn_ref[...]
    sub = lax.broadcasted_iota(jnp.int32, (SUBLANES, LANE), 0)
    tri = (lax.broadcasted_iota(jnp.int32, (SCAN_BLOCK, SCAN_BLOCK), 0)
           >= lax.broadcasted_iota(jnp.int32, (SCAN_BLOCK, SCAN_BLOCK), 1))
    mid = SCAN_BLOCK // 2
    nt = (((1,), (1,)), ((), ()))
    tn = (((0,), (0,)), ((), ()))

    def head_consts(hh):
        lanes = slice(hh * LANE, (hh + 1) * LANE)
        logits = lbl_ref[:, lanes]
        ex = jnp.exp(logits - jnp.max(logits, axis=0, keepdims=True))
        lb = jnp.sum(ex[0:layer + 1, :], axis=0, keepdims=True) / jnp.sum(ex, axis=0, keepdims=True)
        return lb, 1.0 - lb, convw_ref[0:1, lanes], convw_ref[1:2, lanes], convw_ref[2:3, lanes]

    consts = [head_consts(hh) for hh in range(hb)]

    def prefix8(x):
        sh = 1
        while sh < SUBLANES:
            x = x + jnp.where(sub >= sh, pltpu.roll(x, sh, 0), 0.0)
            sh *= 2
        return x

    def conv_block(hh, lo):
        _, _, w0, w1, w2 = consts[hh]
        rows = pl.ds(lo, SCAN_BLOCK)
        cu = cg_ref[hh, rows, :] * u_ref[hh, rows, :]
        cu_ref[hh, pl.ds(SUBLANES + lo, SCAN_BLOCK), :] = cu
        s1 = cu_ref[hh, pl.ds(SUBLANES - 1 + lo, SCAN_BLOCK), :]
        s2 = cu_ref[hh, pl.ds(SUBLANES - 2 + lo, SCAN_BLOCK), :]
        conv = w0 * s2 + w1 * s1 + w2 * cu
        ya_ref[rows, hh * LANE:(hh + 1) * LANE] = (bg_ref[hh, rows, :] * conv).astype(ya_ref.dtype)

    def state_free(hh, lo):
        lb, one_m_lb = consts[hh][0:2]
        rows = pl.ds(lo, SCAN_BLOCK)
        f = lb + one_m_lb * jax.nn.sigmoid(fz_ref[hh, rows, :])
        g2 = jnp.log2(f)
        kj = 1.0 - f
        qj = _silu(q_ref[hh, rows, :])
        vj = iv_ref[hh, rows, :].astype(BF16)
        parts = []
        total = None
        for p in range(SCAN_BLOCK // SUBLANES):
            part = prefix8(g2[p * SUBLANES:(p + 1) * SUBLANES, :])
            if total is not None:
                part = part + total
            total = part[SUBLANES - 1:SUBLANES, :]
            parts.append(part)
        lcj = jnp.concatenate(parts, axis=0)
        ref = lcj[mid - 1:mid, :]
        llast = total
        qd = (qj * jnp.exp2(lcj - ref)).astype(BF16)
        kd = (kj * jnp.exp2(ref - lcj)).astype(BF16)
        att = lax.dot_general(qd, kd, nt, preferred_element_type=F32)
        att = jnp.where(tri, att, 0.0).astype(BF16)
        o_intra = jnp.dot(att, vj, preferred_element_type=F32)
        kdec = (kj * jnp.exp2(llast - lcj)).astype(BF16)
        upd = lax.dot_general(vj, kdec, tn, preferred_element_type=F32)
        return o_intra, upd, jnp.exp2(llast), (qj * jnp.exp2(lcj)).astype(BF16)

    blocks = [(hh, j * SCAN_BLOCK) for hh in range(hb) for j in range(nsb)]
    st = [st_ref[hh] for hh in range(hb)]
    pending = {}
    for n in range(len(blocks) + STATE_LAG):
        if n < len(blocks):
            conv_block(*blocks[n])
            pending[n] = state_free(*blocks[n])
        if n >= STATE_LAG:
            hh, lo = blocks[n - STATE_LAG]
            o_intra, upd, decay, qe = pending.pop(n - STATE_LAG)
            o = o_intra + lax.dot_general(qe, st[hh].astype(BF16), nt, preferred_element_type=F32)
            st[hh] = st[hh] * decay + upd
            rows = pl.ds(lo, SCAN_BLOCK)
            ms = jnp.mean(o * o, axis=-1, keepdims=True)
            yb = o * lax.rsqrt(ms + NORM_EPS) * (hgn * _silu(og_ref[hh, rows, :]))
            yb_ref[rows, hh * LANE:(hh + 1) * LANE] = yb.astype(yb_ref.dtype)
    for hh in range(hb):
        st_ref[hh] = st[hh]
        tail = cu_ref[hh, SUBLANES + tc - (CONV_W - 1):SUBLANES + tc, :]
        cu_ref[hh, SUBLANES - (CONV_W - 1):SUBLANES, :] = tail
        nconv_ref[0, :, hh * LANE:(hh + 1) * LANE] = tail

    @pl.when(c == n_chunks - 1)
    def _fin():
        for hh in range(hb):
            nst_ref[0, hh] = st[hh].T


def _mixer(proj, conv_w, lb_logits, hg_norm, *, row0, nseq, t, tc, hb, n_groups, layer, cache=None, state=None):
    has_cache = cache is not None
    if has_cache != (state is not None):
        raise ValueError("conv cache and recurrent state come together")
    n_chunks = t // tc
    rb0 = row0 // tc
    n_hblk = n_groups // hb

    def pspec(seg):
        return pl.BlockSpec((hb, tc, LANE), lambda b, h, c: (seg * n_hblk + h, rb0 + b * n_chunks + c, 0))

    in_specs = [pspec(s) for s in range(7)]
    in_specs += [
        pl.BlockSpec((CONV_W, hb * LANE), lambda b, h, c: (0, h)),
        pl.BlockSpec((lb_logits.shape[0], hb * LANE), lambda b, h, c: (0, h)),
        pl.BlockSpec((1, LANE), lambda b, h, c: (0, 0)),
    ]
    args = [proj] * 7 + [conv_w, lb_logits, hg_norm]
    if has_cache:
        in_specs += [
            pl.BlockSpec((1, CONV_W - 1, hb * LANE), lambda b, h, c: (b, 0, h)),
            pl.BlockSpec((1, hb, HEAD_DIM, HEAD_DIM), lambda b, h, c: (b, h, 0, 0)),
        ]
        args += [cache, state]
    d_half = n_groups * LANE
    yspec = pl.BlockSpec((tc, hb * LANE), lambda b, h, c: (b * n_chunks + c, h))
    out_specs = [
        yspec, yspec,
        pl.BlockSpec((1, CONV_W - 1, hb * LANE), lambda b, h, c: (b, 0, h)),
        pl.BlockSpec((1, hb, HEAD_DIM, HEAD_DIM), lambda b, h, c: (b, h, 0, 0)),
    ]
    out_shape = [
        jax.ShapeDtypeStruct((nseq * t, d_half), BF16),
        jax.ShapeDtypeStruct((nseq * t, d_half), BF16),
        jax.ShapeDtypeStruct((nseq, CONV_W - 1, d_half), F32),
        jax.ShapeDtypeStruct((nseq, n_groups, HEAD_DIM, HEAD_DIM), F32),
    ]
    kern = functools.partial(_mixer_kernel, tc=tc, hb=hb, n_chunks=n_chunks, has_cache=has_cache,
                             layer=layer)
    return pl.pallas_call(
        kern,
        grid=(nseq, n_hblk, n_chunks),
        in_specs=in_specs,
        out_specs=out_specs,
        out_shape=out_shape,
        scratch_shapes=[pltpu.VMEM((hb, HEAD_DIM, HEAD_DIM), F32),
                        pltpu.VMEM((hb, SUBLANES + tc, LANE), F32)],
        compiler_params=_params(("parallel", "parallel", "arbitrary")),
        name="mixer_cached" if has_cache else "mixer_prompt",
    )(*args)


def _out_proj_kernel(yap_ref, ybp_ref, yas_ref, ybs_ref, w_ref, xp_ref, xs_ref, o_ref, *, n_p):
    i = pl.program_id(1)

    def emit(ya_ref, yb_ref, x_ref):
        y = jnp.concatenate([ya_ref[...], yb_ref[...]], axis=1)
        o_ref[...] = x_ref[...] + jnp.dot(y, w_ref[...], preferred_element_type=F32)

    @pl.when(i < n_p)
    def _():
        emit(yap_ref, ybp_ref, xp_ref)

    @pl.when(i >= n_p)
    def _():
        emit(yas_ref, ybs_ref, xs_ref)


def _out_proj(ya_p, yb_p, ya_s, yb_s, w, xp, xs, tm, tn):
    mp, dh = ya_p.shape
    ms = ya_s.shape[0]
    kdim, n = w.shape
    n_p, n_s = mp // tm, ms // tm

    def prow(j, i):
        return jnp.minimum(i, n_p - 1)

    def srow(j, i):
        return jnp.maximum(i - n_p, 0)

    return pl.pallas_call(
        functools.partial(_out_proj_kernel, n_p=n_p),
        grid=(n // tn, n_p + n_s),
        in_specs=[
            pl.BlockSpec((tm, dh), lambda j, i: (prow(j, i), 0)),
            pl.BlockSpec((tm, dh), lambda j, i: (prow(j, i), 0)),
            pl.BlockSpec((tm, dh), lambda j, i: (srow(j, i), 0)),
            pl.BlockSpec((tm, dh), lambda j, i: (srow(j, i), 0)),
            pl.BlockSpec((kdim, tn), lambda j, i: (0, j)),
            pl.BlockSpec((tm, tn), lambda j, i: (prow(j, i), j)),
            pl.BlockSpec((tm, tn), lambda j, i: (srow(j, i), j)),
        ],
        out_specs=pl.BlockSpec((tm, tn), lambda j, i: (i, j)),
        out_shape=jax.ShapeDtypeStruct((mp + ms, n), F32),
        compiler_params=_params(("parallel", "parallel")),
        name="out_proj",
    )(ya_p, yb_p, ya_s, yb_s, w, xp, xs)


def _ffn_up_kernel(a_ref, wg_ref, wu_ref, o_ref, *, n_valid):
    j = pl.program_id(1)

    @pl.when(j < n_valid)
    def _():
        a = a_ref[...]
        g = jnp.dot(a, wg_ref[...].astype(BF16), preferred_element_type=F32)
        u = jnp.dot(a, wu_ref[...].astype(BF16), preferred_element_type=F32)
        o_ref[...] = (_silu(g) * u).astype(o_ref.dtype)

    @pl.when(j >= n_valid)
    def _():
        o_ref[...] = jnp.zeros_like(o_ref)


def _ffn_up(h, wg, wu, tm, tn, f_pad):
    m, k = h.shape
    n_valid = wg.shape[1] // tn
    wspec = pl.BlockSpec((k, tn), lambda i, j: (0, jnp.minimum(j, n_valid - 1)))
    return pl.pallas_call(
        functools.partial(_ffn_up_kernel, n_valid=n_valid),
        grid=(m // tm, f_pad // tn),
        in_specs=[pl.BlockSpec((tm, k), lambda i, j: (i, 0)), wspec, wspec],
        out_specs=pl.BlockSpec((tm, tn), lambda i, j: (i, j)),
        out_shape=jax.ShapeDtypeStruct((m, f_pad), BF16),
        compiler_params=_params(("parallel", "parallel")),
        name="ffn_up",
    )(h, wg, wu)


def _ffn_down_kernel(a_ref, w_ref, x_ref, o_ref, *, k_valid):
    rows = lax.broadcasted_iota(jnp.int32, w_ref.shape, 0)
    w = jnp.where(rows < k_valid, w_ref[...], jnp.zeros_like(w_ref))
    o_ref[...] = x_ref[...] + jnp.dot(a_ref[...], w, preferred_element_type=F32)


def _ffn_down(act, w, x1, tm, tn):
    m, kf = act.shape
    n = w.shape[1]
    return pl.pallas_call(
        functools.partial(_ffn_down_kernel, k_valid=w.shape[0]),
        grid=(m // tm, n // tn),
        in_specs=[
            pl.BlockSpec((tm, kf), lambda i, j: (i, 0)),
            pl.BlockSpec((kf, tn), lambda i, j: (0, j)),
            pl.BlockSpec((tm, tn), lambda i, j: (i, j)),
        ],
        out_specs=pl.BlockSpec((tm, tn), lambda i, j: (i, j)),
        out_shape=jax.ShapeDtypeStruct((m, n), F32),
        compiler_params=_params(("parallel", "parallel")),
        name="ffn_down",
    )(act, w, x1)


def _mixer_tiles(t, n_groups):
    tc = _largest_tile(t, MIXER_BLOCKS_PER_STEP * SCAN_BLOCK, SCAN_BLOCK)
    hb = 1
    while hb < n_groups and 2 * hb * tc <= MIXER_BLOCKS_PER_STEP * SCAN_BLOCK and n_groups % (2 * hb) == 0:
        hb *= 2
    return dict(tc=tc, hb=hb)


def _tiles(mp, ms, tp, ts, n_groups, d, n_in, d_ff, f_pad):
    m = mp + ms
    tr = _largest_tile(_gcd(mp, ms), 512, BF16_ROWS)
    return dict(
        tr=tr,
        tm=_largest_tile(m, 1056, BF16_ROWS),
        tn_in=_largest_tile(n_in, 512, LANE),
        tn_out=_largest_tile(d, 1024, LANE),
        tn_up=_largest_tile(_gcd(d_ff, f_pad), 256, LANE),
        tm_down=_largest_tile(m, 528, BF16_ROWS),
        tn_down=_largest_tile(d, 512, LANE),
        mix_p=_mixer_tiles(tp, n_groups),
        mix_s=_mixer_tiles(ts, n_groups),
    )


def _gcd(a, b):
    while b:
        a, b = b, a % b
    return a


def kernel(x_prompt, x_sample, cache_conv, state_hgrn, norm_mix, w_in, conv_w, lb_logits, hg_norm,
           w_out, norm_ffn, w_gate, w_up, w_down, norm_final):
    bp, tp, d = x_prompt.shape
    bs, ts, _ = x_sample.shape
    depth = w_in.shape[0]
    if depth != 1:
        raise NotImplementedError("one layer")
    layer = 0
    n_in = w_in.shape[2]
    d_ff = w_gate.shape[2]
    d_half = conv_w.shape[2]
    n_groups = d_half // LANE
    if n_in != 7 * d_half or state_hgrn.shape[2] != n_groups or w_out.shape[1] != 2 * d_half:
        raise ValueError("unexpected layer geometry")
    mp, ms = bp * tp, bs * ts
    m = mp + ms
    f_pad = _round_up(d_ff, 1024)
    t = _tiles(mp, ms, tp, ts, n_groups, d, n_in, d_ff, f_pad)

    xp = x_prompt.reshape(mp, d)
    xs = x_sample.reshape(ms, d)
    w_out_b = w_out[layer].astype(BF16)
    w_down_b = w_down[layer].astype(BF16)

    h = _norm_two_src(xp, xs, norm_mix[layer].reshape(1, d), t["tr"])
    proj = _in_proj(h, w_in[layer], t["tm"], t["tn_in"])

    mix_args = dict(n_groups=n_groups, layer=layer)
    hgn = hg_norm[layer].reshape(1, HEAD_DIM)
    ya_p, yb_p, conv_p, hg_p = _mixer(proj, conv_w[layer], lb_logits, hgn, row0=0, nseq=bp, t=tp,
                                      **t["mix_p"], **mix_args)
    ya_s, yb_s, conv_s, hg_s = _mixer(proj, conv_w[layer], lb_logits, hgn, row0=mp, nseq=bs, t=ts,
                                      cache=cache_conv[layer], state=state_hgrn[layer],
                                      **t["mix_s"], **mix_args)

    x1 = _out_proj(ya_p, yb_p, ya_s, yb_s, w_out_b, xp, xs, t["tr"], t["tn_out"])
    h2 = _norm(x1, norm_ffn[layer].reshape(1, d), t["tr"])
    act = _ffn_up(h2, w_gate[layer], w_up[layer], t["tm"], t["tn_up"], f_pad)
    x2 = _ffn_down(act, w_down_b, x1, t["tm_down"], t["tn_down"])
    y_p, y_s = _norm_two_sink(x2, norm_final.reshape(1, d), mp, t["tr"])

    return (y_p.reshape(bp, tp, d), y_s.reshape(bs, ts, d),
            conv_p[None], hg_p[None].astype(x_prompt.dtype),
            conv_s[None].astype(cache_conv.dtype), hg_s[None].astype(state_hgrn.dtype))
```

```python
import functools

import jax
import jax.numpy as jnp
from jax import lax
from jax.experimental import pallas as pl
from jax.experimental.pallas import tpu as pltpu

LANE = 128
SUBLANES = 8
BF16_ROWS = 16
SCAN_BLOCK = 16
HEAD_DIM = 128
CONV_W = 3
NORM_EPS = 1e-6
V7X_VMEM_BYTES = 64 * 1024 * 1024
VMEM_LIMIT = V7X_VMEM_BYTES - 6 * 1024 * 1024

F32 = jnp.float32
BF16 = jnp.bfloat16


def _largest_tile(n, cap, quantum):
    best = None
    for t in range(quantum, min(n, cap) + 1, quantum):
        if n % t == 0:
            best = t
    if best is None:
        raise ValueError(f"no tile for n={n} cap={cap} quantum={quantum}")
    return best


def _round_up(n, m):
    return (n + m - 1) // m * m


def _params(sem):
    return pltpu.CompilerParams(dimension_semantics=sem, vmem_limit_bytes=VMEM_LIMIT)


def _rms_rows(x, w):
    ms = jnp.mean(x * x, axis=-1, keepdims=True)
    return x * lax.rsqrt(ms + NORM_EPS) * w


def _silu(x):
    return x * jax.nn.sigmoid(x)


def _norm_two_src_kernel(xp_ref, xs_ref, w_ref, o_ref, *, n_p):
    i = pl.program_id(0)

    @pl.when(i < n_p)
    def _():
        o_ref[...] = _rms_rows(xp_ref[...], w_ref[...]).astype(o_ref.dtype)

    @pl.when(i >= n_p)
    def _():
        o_ref[...] = _rms_rows(xs_ref[...], w_ref[...]).astype(o_ref.dtype)


def _norm_two_src(xp, xs, w, tr):
    mp, d = xp.shape
    ms = xs.shape[0]
    n_p, n_s = mp // tr, ms // tr
    return pl.pallas_call(
        functools.partial(_norm_two_src_kernel, n_p=n_p),
        grid=(n_p + n_s,),
        in_specs=[
            pl.BlockSpec((tr, d), lambda i: (jnp.minimum(i, n_p - 1), 0)),
            pl.BlockSpec((tr, d), lambda i: (jnp.maximum(i - n_p, 0), 0)),
            pl.BlockSpec((1, d), lambda i: (0, 0)),
        ],
        out_specs=pl.BlockSpec((tr, d), lambda i: (i, 0)),
        out_shape=jax.ShapeDtypeStruct((mp + ms, d), BF16),
        compiler_params=_params(("parallel",)),
        name="norm1",
    )(xp, xs, w)


def _norm_kernel(x_ref, w_ref, o_ref):
    o_ref[...] = _rms_rows(x_ref[...], w_ref[...]).astype(o_ref.dtype)


def _norm(x, w, tr):
    m, d = x.shape
    return pl.pallas_call(
        _norm_kernel,
        grid=(m // tr,),
        in_specs=[pl.BlockSpec((tr, d), lambda i: (i, 0)), pl.BlockSpec((1, d), lambda i: (0, 0))],
        out_specs=pl.BlockSpec((tr, d), lambda i: (i, 0)),
        out_shape=jax.ShapeDtypeStruct((m, d), BF16),
        compiler_params=_params(("parallel",)),
        name="norm2",
    )(x, w)


def _norm_two_sink_kernel(x_ref, w_ref, op_ref, os_ref, *, n_p):
    i = pl.program_id(0)

    @pl.when(i < n_p)
    def _():
        op_ref[...] = _rms_rows(x_ref[...], w_ref[...])

    @pl.when(i >= n_p)
    def _():
        os_ref[...] = _rms_rows(x_ref[...], w_ref[...])


def _norm_two_sink(x, w, mp, tr):
    m, d = x.shape
    ms = m - mp
    n_p, n_s = mp // tr, ms // tr
    return pl.pallas_call(
        functools.partial(_norm_two_sink_kernel, n_p=n_p),
        grid=(n_p + n_s,),
        in_specs=[pl.BlockSpec((tr, d), lambda i: (i, 0)), pl.BlockSpec((1, d), lambda i: (0, 0))],
        out_specs=[
            pl.BlockSpec((tr, d), lambda i: (jnp.minimum(i, n_p - 1), 0)),
            pl.BlockSpec((tr, d), lambda i: (jnp.maximum(i - n_p, 0), 0)),
        ],
        out_shape=[jax.ShapeDtypeStruct((mp, d), F32), jax.ShapeDtypeStruct((ms, d), F32)],
        compiler_params=_params(("arbitrary",)),
        name="final_norm",
    )(x, w)


def _in_proj_kernel(a_ref, b_ref, o_ref):
    acc = jnp.dot(a_ref[...], b_ref[...].astype(BF16), preferred_element_type=F32)
    for g in range(o_ref.shape[0]):
        o_ref[g] = acc[:, g * LANE:(g + 1) * LANE]


def _in_proj(h, w, tm, tn):
    m, k = h.shape
    n = w.shape[1]
    return pl.pallas_call(
        _in_proj_kernel,
        grid=(m // tm, n // tn),
        in_specs=[pl.BlockSpec((tm, k), lambda i, j: (i, 0)), pl.BlockSpec((k, tn), lambda i, j: (0, j))],
        out_specs=pl.BlockSpec((tn // LANE, tm, LANE), lambda i, j: (j, i, 0)),
        out_shape=jax.ShapeDtypeStruct((n // LANE, m, LANE), F32),
        compiler_params=_params(("parallel", "parallel")),
        name="in_proj",
    )(h, w)


STATE_LAG = 4
MIXER_BLOCKS_PER_STEP = 256


def _mixer_kernel(*refs, tc, hb, n_chunks, has_cache, layer):
    if has_cache:
        (bg_ref, cg_ref, u_ref, q_ref, fz_ref, iv_ref, og_ref, convw_ref, lbl_ref, hgn_ref,
         cbuf_ref, s0_ref, ya_ref, yb_ref, nconv_ref, nst_ref, st_ref, cu_ref) = refs
    else:
        (bg_ref, cg_ref, u_ref, q_ref, fz_ref, iv_ref, og_ref, convw_ref, lbl_ref, hgn_ref,
         ya_ref, yb_ref, nconv_ref, nst_ref, st_ref, cu_ref) = refs
    c = pl.program_id(2)
    nsb = tc // SCAN_BLOCK

    @pl.when(c == 0)
    def _init():
        for hh in range(hb):
            if has_cache:
                st_ref[hh] = s0_ref[0, hh].T
                cu_ref[hh, SUBLANES - (CONV_W - 1):SUBLANES, :] = cbuf_ref[0, :, hh * LANE:(hh + 1) * LANE]
            else:
                st_ref[hh] = jnp.zeros((HEAD_DIM, HEAD_DIM), F32)
                cu_ref[hh, 0:SUBLANES, :] = jnp.zeros((SUBLANES, LANE), F32)

    hgn = hgn_ref[...]
    sub = lax.broadcasted_iota(jnp.int32, (SUBLANES, LANE), 0)
    tri = (lax.broadcasted_iota(jnp.int32, (SCAN_BLOCK, SCAN_BLOCK), 0)
           >= lax.broadcasted_iota(jnp.int32, (SCAN_BLOCK, SCAN_BLOCK), 1))
    mid = SCAN_BLOCK // 2
    nt = (((1,), (1,)), ((), ()))
    tn = (((0,), (0,)), ((), ()))

    def head_consts(hh):
        lanes = slice(hh * LANE, (hh + 1) * LANE)
        logits = lbl_ref[:, lanes]
        ex = jnp.exp(logits - jnp.max(logits, axis=0, keepdims=True))
        lb = jnp.sum(ex[0:layer + 1, :], axis=0, keepdims=True) / jnp.sum(ex, axis=0, keepdims=True)
        return lb, 1.0 - lb, convw_ref[0:1, lanes], convw_ref[1:2, lanes], convw_ref[2:3, lanes]

    consts = [head_consts(hh) for hh in range(hb)]

    def prefix8(x):
        sh = 1
        while sh < SUBLANES:
            x = x + jnp.where(sub >= sh, pltpu.roll(x, sh, 0), 0.0)
            sh *= 2
        return x

    def conv_block(hh, lo):
        _, _, w0, w1, w2 = consts[hh]
        rows = pl.ds(lo, SCAN_BLOCK)
        cu = cg_ref[hh, rows, :] * u_ref[hh, rows, :]
        cu_ref[hh, pl.ds(SUBLANES + lo, SCAN_BLOCK), :] = cu
        s1 = cu_ref[hh, pl.ds(SUBLANES - 1 + lo, SCAN_BLOCK), :]
        s2 = cu_ref[hh, pl.ds(SUBLANES - 2 + lo, SCAN_BLOCK), :]
        conv = w0 * s2 + w1 * s1 + w2 * cu
        ya_ref[rows, hh * LANE:(hh + 1) * LANE] = (bg_ref[hh, rows, :] * conv).astype(ya_ref.dtype)

    def state_free(hh, lo):
        lb, one_m_lb = consts[hh][0:2]
        rows = pl.ds(lo, SCAN_BLOCK)
        f = lb + one_m_lb * jax.nn.sigmoid(fz_ref[hh, rows, :])
        g2 = jnp.log2(f)
        kj = 1.0 - f
        qj = _silu(q_ref[hh, rows, :])
        vj = iv_ref[hh, rows, :].astype(BF16)
        parts = []
        total = None
        for p in range(SCAN_BLOCK // SUBLANES):
            part = prefix8(g2[p * SUBLANES:(p + 1) * SUBLANES, :])
            if total is not None:
                part = part + total
            total = part[SUBLANES - 1:SUBLANES, :]
            parts.append(part)
        lcj = jnp.concatenate(parts, axis=0)
        ref = lcj[mid - 1:mid, :]
        llast = total
        qd = (qj * jnp.exp2(lcj - ref)).astype(BF16)
        kd = (kj * jnp.exp2(ref - lcj)).astype(BF16)
        att = lax.dot_general(qd, kd, nt, preferred_element_type=F32)
        att = jnp.where(tri, att, 0.0).astype(BF16)
        o_intra = jnp.dot(att, vj, preferred_element_type=F32)
        kdec = (kj * jnp.exp2(llast - lcj)).astype(BF16)
        upd = lax.dot_general(vj, kdec, tn, preferred_element_type=F32)
        return o_intra, upd, jnp.exp2(llast), (qj * jnp.exp2(lcj)).astype(BF16)

    blocks = [(hh, j * SCAN_BLOCK) for hh in range(hb) for j in range(nsb)]
    st = [st_ref[hh] for hh in range(hb)]
    pending = {}
    for n in range(len(blocks) + STATE_LAG):
        if n < len(blocks):
            conv_block(*blocks[n])
            pending[n] = state_free(*blocks[n])
        if n >= STATE_LAG:
            hh, lo = blocks[n - STATE_LAG]
            o_intra, upd, decay, qe = pending.pop(n - STATE_LAG)
            o = o_intra + lax.dot_general(qe, st[hh].astype(BF16), nt, preferred_element_type=F32)
            st[hh] = st[hh] * decay + upd
            rows = pl.ds(lo, SCAN_BLOCK)
            ms = jnp.mean(o * o, axis=-1, keepdims=True)
            yb = o * lax.rsqrt(ms + NORM_EPS) * (hgn * _silu(og_ref[hh, rows, :]))
            yb_ref[rows, hh * LANE:(hh + 1) * LANE] = yb.astype(yb_ref.dtype)
    for hh in range(hb):
        st_ref[hh] = st[hh]
        tail = cu_ref[hh, SUBLANES + tc - (CONV_W - 1):SUBLANES + tc, :]
        cu_ref[hh, SUBLANES - (CONV_W - 1):SUBLANES, :] = tail
        nconv_ref[0, :, hh * LANE:(hh + 1) * LANE] = tail

    @pl.when(c == n_chunks - 1)
    def _fin():
        for hh in range(hb):
            nst_ref[0, hh] = st[hh].T


def _mixer(proj, conv_w, lb_logits, hg_norm, *, row0, nseq, t, tc, hb, n_groups, layer, cache=None, state=None):
    has_cache = cache is not None
    if has_cache != (state is not None):
        raise ValueError("conv cache and recurrent state come together")
    n_chunks = t // tc
    rb0 = row0 // tc
    n_hblk = n_groups // hb

    def pspec(seg):
        return pl.BlockSpec((hb, tc, LANE), lambda b, h, c: (seg * n_hblk + h, rb0 + b * n_chunks + c, 0))

    in_specs = [pspec(s) for s in range(7)]
    in_specs += [
        pl.BlockSpec((CONV_W, hb * LANE), lambda b, h, c: (0, h)),
        pl.BlockSpec((lb_logits.shape[0], hb * LANE), lambda b, h, c: (0, h)),
        pl.BlockSpec((1, LANE), lambda b, h, c: (0, 0)),
    ]
    args = [proj] * 7 + [conv_w, lb_logits, hg_norm]
    if has_cache:
        in_specs += [
            pl.BlockSpec((1, CONV_W - 1, hb * LANE), lambda b, h, c: (b, 0, h)),
            pl.BlockSpec((1, hb, HEAD_DIM, HEAD_DIM), lambda b, h, c: (b, h, 0, 0)),
        ]
        args += [cache, state]
    d_half = n_groups * LANE
    yspec = pl.BlockSpec((tc, hb * LANE), lambda b, h, c: (b * n_chunks + c, h))
    out_specs = [
        yspec, yspec,
        pl.BlockSpec((1, CONV_W - 1, hb * LANE), lambda b, h, c: (b, 0, h)),
        pl.BlockSpec((1, hb, HEAD_DIM, HEAD_DIM), lambda b, h, c: (b, h, 0, 0)),
    ]
    out_shape = [
        jax.ShapeDtypeStruct((nseq * t, d_half), BF16),
        jax.ShapeDtypeStruct((nseq * t, d_half), BF16),
        jax.ShapeDtypeStruct((nseq, CONV_W - 1, d_half), F32),
        jax.ShapeDtypeStruct((nseq, n_groups, HEAD_DIM, HEAD_DIM), F32),
    ]
    kern = functools.partial(_mixer_kernel, tc=tc, hb=hb, n_chunks=n_chunks, has_cache=has_cache,
                             layer=layer)
    return pl.pallas_call(
        kern,
        grid=(nseq, n_hblk, n_chunks),
        in_specs=in_specs,
        out_specs=out_specs,
        out_shape=out_shape,
        scratch_shapes=[pltpu.VMEM((hb, HEAD_DIM, HEAD_DIM), F32),
                        pltpu.VMEM((hb, SUBLANES + tc, LANE), F32)],
        compiler_params=_params(("parallel", "parallel", "arbitrary")),
        name="mixer_cached" if has_cache else "mixer_prompt",
    )(*args)


def _out_proj_kernel(yap_ref, ybp_ref, yas_ref, ybs_ref, w_ref, xp_ref, xs_ref, o_ref, *, n_p):
    i = pl.program_id(1)

    def emit(ya_ref, yb_ref, x_ref):
        y = jnp.concatenate([ya_ref[...], yb_ref[...]], axis=1)
        o_ref[...] = x_ref[...] + jnp.dot(y, w_ref[...], preferred_element_type=F32)

    @pl.when(i < n_p)
    def _():
        emit(yap_ref, ybp_ref, xp_ref)

    @pl.when(i >= n_p)
    def _():
        emit(yas_ref, ybs_ref, xs_ref)


def _out_proj(ya_p, yb_p, ya_s, yb_s, w, xp, xs, tm, tn):
    mp, dh = ya_p.shape
    ms = ya_s.shape[0]
    kdim, n = w.shape
    n_p, n_s = mp // tm, ms // tm

    def prow(j, i):
        return jnp.minimum(i, n_p - 1)

    def srow(j, i):
        return jnp.maximum(i - n_p, 0)

    return pl.pallas_call(
        functools.partial(_out_proj_kernel, n_p=n_p),
        grid=(n // tn, n_p + n_s),
        in_specs=[
            pl.BlockSpec((tm, dh), lambda j, i: (prow(j, i), 0)),
            pl.BlockSpec((tm, dh), lambda j, i: (prow(j, i), 0)),
            pl.BlockSpec((tm, dh), lambda j, i: (srow(j, i), 0)),
            pl.BlockSpec((tm, dh), lambda j, i: (srow(j, i), 0)),
            pl.BlockSpec((kdim, tn), lambda j, i: (0, j)),
            pl.BlockSpec((tm, tn), lambda j, i: (prow(j, i), j)),
            pl.BlockSpec((tm, tn), lambda j, i: (srow(j, i), j)),
        ],
        out_specs=pl.BlockSpec((tm, tn), lambda j, i: (i, j)),
        out_shape=jax.ShapeDtypeStruct((mp + ms, n), F32),
        compiler_params=_params(("parallel", "parallel")),
        name="out_proj",
    )(ya_p, yb_p, ya_s, yb_s, w, xp, xs)


def _ffn_up_kernel(a_ref, wg_ref, wu_ref, o_ref, *, n_valid):
    j = pl.program_id(1)

    @pl.when(j < n_valid)
    def _():
        a = a_ref[...]
        g = jnp.dot(a, wg_ref[...].astype(BF16), preferred_element_type=F32)
        u = jnp.dot(a, wu_ref[...].astype(BF16), preferred_element_type=F32)
        o_ref[...] = (_silu(g) * u).astype(o_ref.dtype)

    @pl.when(j >= n_valid)
    def _():
        o_ref[...] = jnp.zeros_like(o_ref)


def _ffn_up(h, wg, wu, tm, tn, f_pad):
    m, k = h.shape
    n_valid = wg.shape[1] // tn
    wspec = pl.BlockSpec((k, tn), lambda i, j: (0, jnp.minimum(j, n_valid - 1)))
    return pl.pallas_call(
        functools.partial(_ffn_up_kernel, n_valid=n_valid),
        grid=(m // tm, f_pad // tn),
        in_specs=[pl.BlockSpec((tm, k), lambda i, j: (i, 0)), wspec, wspec],
        out_specs=pl.BlockSpec((tm, tn), lambda i, j: (i, j)),
        out_shape=jax.ShapeDtypeStruct((m, f_pad), BF16),
        compiler_params=_params(("parallel", "parallel")),
        name="ffn_up",
    )(h, wg, wu)


def _ffn_down_kernel(a_ref, w_ref, x_ref, o_ref, *, k_valid):
    rows = lax.broadcasted_iota(jnp.int32, w_ref.shape, 0)
    w = jnp.where(rows < k_valid, w_ref[...], jnp.zeros_like(w_ref))
    o_ref[...] = x_ref[...] + jnp.dot(a_ref[...], w, preferred_element_type=F32)


def _ffn_down(act, w, x1, tm, tn):
    m, kf = act.shape
    n = w.shape[1]
    return pl.pallas_call(
        functools.partial(_ffn_down_kernel, k_valid=w.shape[0]),
        grid=(m // tm, n // tn),
        in_specs=[
            pl.BlockSpec((tm, kf), lambda i, j: (i, 0)),
            pl.BlockSpec((kf, tn), lambda i, j: (0, j)),
            pl.BlockSpec((tm, tn), lambda i, j: (i, j)),
        ],
        out_specs=pl.BlockSpec((tm, tn), lambda i, j: (i, j)),
        out_shape=jax.ShapeDtypeStruct((m, n), F32),
        compiler_params=_params(("parallel", "parallel")),
        name="ffn_down",
    )(act, w, x1)


def _mixer_tiles(t, n_groups):
    tc = _largest_tile(t, MIXER_BLOCKS_PER_STEP * SCAN_BLOCK, SCAN_BLOCK)
    hb = 1
    while hb < n_groups and 2 * hb * tc <= MIXER_BLOCKS_PER_STEP * SCAN_BLOCK and n_groups % (2 * hb) == 0:
        hb *= 2
    return dict(tc=tc, hb=hb)


def _tiles(mp, ms, tp, ts, n_groups, d, n_in, d_ff, f_pad):
    m = mp + ms
    tr = _largest_tile(_gcd(mp, ms), 512, BF16_ROWS)
    return dict(
        tr=tr,
        tm=_largest_tile(m, 1536, BF16_ROWS),
        tn_in=_largest_tile(n_in, 512, LANE),
        tn_out=_largest_tile(d, 1024, LANE),
        tn_up=_largest_tile(_gcd(d_ff, f_pad), 256, LANE),
        tm_down=_largest_tile(m, 528, BF16_ROWS),
        tn_down=_largest_tile(d, 512, LANE),
        mix_p=_mixer_tiles(tp, n_groups),
        mix_s=_mixer_tiles(ts, n_groups),
    )


def _gcd(a, b):
    while b:
        a, b = b, a % b
    return a


def kernel(x_prompt, x_sample, cache_conv, state_hgrn, norm_mix, w_in, conv_w, lb_logits, hg_norm,
           w_out, norm_ffn, w_gate, w_up, w_down, norm_final):
    bp, tp, d = x_prompt.shape
    bs, ts, _ = x_sample.shape
    depth = w_in.shape[0]
    if depth != 1:
        raise NotImplementedError("one layer")
    layer = 0
    n_in = w_in.shape[2]
    d_ff = w_gate.shape[2]
    d_half = conv_w.shape[2]
    n_groups = d_half // LANE
    if n_in != 7 * d_half or state_hgrn.shape[2] != n_groups or w_out.shape[1] != 2 * d_half:
        raise ValueError("unexpected layer geometry")
    mp, ms = bp * tp, bs * ts
    m = mp + ms
    f_pad = _round_up(d_ff, 1024)
    t = _tiles(mp, ms, tp, ts, n_groups, d, n_in, d_ff, f_pad)

    xp = x_prompt.reshape(mp, d)
    xs = x_sample.reshape(ms, d)
    w_out_b = w_out[layer].astype(BF16)
    w_down_b = w_down[layer].astype(BF16)

    h = _norm_two_src(xp, xs, norm_mix[layer].reshape(1, d), t["tr"])
    proj = _in_proj(h, w_in[layer], t["tm"], t["tn_in"])

    mix_args = dict(n_groups=n_groups, layer=layer)
    hgn = hg_norm[layer].reshape(1, HEAD_DIM)
    ya_p, yb_p, conv_p, hg_p = _mixer(proj, conv_w[layer], lb_logits, hgn, row0=0, nseq=bp, t=tp,
                                      **t["mix_p"], **mix_args)
    ya_s, yb_s, conv_s, hg_s = _mixer(proj, conv_w[layer], lb_logits, hgn, row0=mp, nseq=bs, t=ts,
                                      cache=cache_conv[layer], state=state_hgrn[layer],
                                      **t["mix_s"], **mix_args)

    x1 = _out_proj(ya_p, yb_p, ya_s, yb_s, w_out_b, xp, xs, t["tr"], t["tn_out"])
    h2 = _norm(x1, norm_ffn[layer].reshape(1, d), t["tr"])
    act = _ffn_up(h2, w_gate[layer], w_up[layer], t["tm"], t["tn_up"], f_pad)
    x2 = _ffn_down(act, w_down_b, x1, t["tm_down"], t["tn_down"])
    y_p, y_s = _norm_two_sink(x2, norm_final.reshape(1, d), mp, t["tr"])

    return (y_p.reshape(bp, tp, d), y_s.reshape(bs, ts, d),
            conv_p[None], hg_p[None].astype(x_prompt.dtype),
            conv_s[None].astype(cache_conv.dtype), hg_s[None].astype(state_hgrn.dtype))
```

```python
import functools

import jax
import jax.numpy as jnp
from jax import lax
from jax.experimental import pallas as pl
from jax.experimental.pallas import tpu as pltpu

LANE = 128
SUBLANES = 8
BF16_ROWS = 16
SCAN_BLOCK = 16
HEAD_DIM = 128
CONV_W = 3
NORM_EPS = 1e-6
V7X_VMEM_BYTES = 64 * 1024 * 1024
VMEM_LIMIT = V7X_VMEM_BYTES - 6 * 1024 * 1024

F32 = jnp.float32
BF16 = jnp.bfloat16


def _largest_tile(n, cap, quantum):
    best = None
    for t in range(quantum, min(n, cap) + 1, quantum):
        if n % t == 0:
            best = t
    if best is None:
        raise ValueError(f"no tile for n={n} cap={cap} quantum={quantum}")
    return best


def _round_up(n, m):
    return (n + m - 1) // m * m


def _params(sem):
    return pltpu.CompilerParams(dimension_semantics=sem, vmem_limit_bytes=VMEM_LIMIT)


def _rms_rows(x, w):
    ms = jnp.mean(x * x, axis=-1, keepdims=True)
    return x * lax.rsqrt(ms + NORM_EPS) * w


def _silu(x):
    return x * jax.nn.sigmoid(x)


def _norm_two_src_kernel(xp_ref, xs_ref, w_ref, o_ref, *, n_p):
    i = pl.program_id(0)

    @pl.when(i < n_p)
    def _():
        o_ref[...] = _rms_rows(xp_ref[...], w_ref[...]).astype(o_ref.dtype)

    @pl.when(i >= n_p)
    def _():
        o_ref[...] = _rms_rows(xs_ref[...], w_ref[...]).astype(o_ref.dtype)


def _norm_two_src(xp, xs, w, tr):
    mp, d = xp.shape
    ms = xs.shape[0]
    n_p, n_s = mp // tr, ms // tr
    return pl.pallas_call(
        functools.partial(_norm_two_src_kernel, n_p=n_p),
        grid=(n_p + n_s,),
        in_specs=[
            pl.BlockSpec((tr, d), lambda i: (jnp.minimum(i, n_p - 1), 0)),
            pl.BlockSpec((tr, d), lambda i: (jnp.maximum(i - n_p, 0), 0)),
            pl.BlockSpec((1, d), lambda i: (0, 0)),
        ],
        out_specs=pl.BlockSpec((tr, d), lambda i: (i, 0)),
        out_shape=jax.ShapeDtypeStruct((mp + ms, d), BF16),
        compiler_params=_params(("parallel",)),
        name="norm1",
    )(xp, xs, w)


def _norm_kernel(x_ref, w_ref, o_ref):
    o_ref[...] = _rms_rows(x_ref[...], w_ref[...]).astype(o_ref.dtype)


def _norm(x, w, tr):
    m, d = x.shape
    return pl.pallas_call(
        _norm_kernel,
        grid=(m // tr,),
        in_specs=[pl.BlockSpec((tr, d), lambda i: (i, 0)), pl.BlockSpec((1, d), lambda i: (0, 0))],
        out_specs=pl.BlockSpec((tr, d), lambda i: (i, 0)),
        out_shape=jax.ShapeDtypeStruct((m, d), BF16),
        compiler_params=_params(("parallel",)),
        name="norm2",
    )(x, w)


def _norm_two_sink_kernel(x_ref, w_ref, op_ref, os_ref, *, n_p):
    i = pl.program_id(0)

    @pl.when(i < n_p)
    def _():
        op_ref[...] = _rms_rows(x_ref[...], w_ref[...])

    @pl.when(i >= n_p)
    def _():
        os_ref[...] = _rms_rows(x_ref[...], w_ref[...])


def _norm_two_sink(x, w, mp, tr):
    m, d = x.shape
    ms = m - mp
    n_p, n_s = mp // tr, ms // tr
    return pl.pallas_call(
        functools.partial(_norm_two_sink_kernel, n_p=n_p),
        grid=(n_p + n_s,),
        in_specs=[pl.BlockSpec((tr, d), lambda i: (i, 0)), pl.BlockSpec((1, d), lambda i: (0, 0))],
        out_specs=[
            pl.BlockSpec((tr, d), lambda i: (jnp.minimum(i, n_p - 1), 0)),
            pl.BlockSpec((tr, d), lambda i: (jnp.maximum(i - n_p, 0), 0)),
        ],
        out_shape=[jax.ShapeDtypeStruct((mp, d), F32), jax.ShapeDtypeStruct((ms, d), F32)],
        compiler_params=_params(("arbitrary",)),
        name="final_norm",
    )(x, w)


def _in_proj_kernel(a_ref, b_ref, o_ref):
    acc = jnp.dot(a_ref[...], b_ref[...].astype(BF16), preferred_element_type=F32)
    for g in range(o_ref.shape[0]):
        o_ref[g] = acc[:, g * LANE:(g + 1) * LANE]


def _in_proj(h, w, tm, tn):
    m, k = h.shape
    n = w.shape[1]
    return pl.pallas_call(
        _in_proj_kernel,
        grid=(n // tn, m // tm),
        in_specs=[pl.BlockSpec((tm, k), lambda j, i: (i, 0)), pl.BlockSpec((k, tn), lambda j, i: (0, j))],
        out_specs=pl.BlockSpec((tn // LANE, tm, LANE), lambda j, i: (j, i, 0)),
        out_shape=jax.ShapeDtypeStruct((n // LANE, m, LANE), F32),
        compiler_params=_params(("parallel", "parallel")),
        name="in_proj",
    )(h, w)


STATE_LAG = 4
MIXER_BLOCKS_PER_STEP = 256


def _mixer_kernel(*refs, tc, hb, n_chunks, has_cache, layer):
    if has_cache:
        (bg_ref, cg_ref, u_ref, q_ref, fz_ref, iv_ref, og_ref, convw_ref, lbl_ref, hgn_ref,
         cbuf_ref, s0_ref, ya_ref, yb_ref, nconv_ref, nst_ref, st_ref, cu_ref) = refs
    else:
        (bg_ref, cg_ref, u_ref, q_ref, fz_ref, iv_ref, og_ref, convw_ref, lbl_ref, hgn_ref,
         ya_ref, yb_ref, nconv_ref, nst_ref, st_ref, cu_ref) = refs
    c = pl.program_id(2)
    nsb = tc // SCAN_BLOCK

    @pl.when(c == 0)
    def _init():
        for hh in range(hb):
            if has_cache:
                st_ref[hh] = s0_ref[0, hh].T
                cu_ref[hh, SUBLANES - (CONV_W - 1):SUBLANES, :] = cbuf_ref[0, :, hh * LANE:(hh + 1) * LANE]
            else:
                st_ref[hh] = jnp.zeros((HEAD_DIM, HEAD_DIM), F32)
                cu_ref[hh, 0:SUBLANES, :] = jnp.zeros((SUBLANES, LANE), F32)

    hgn = hgn_ref[...]
    sub = lax.broadcasted_iota(jnp.int32, (SUBLANES, LANE), 0)
    tri = (lax.broadcasted_iota(jnp.int32, (SCAN_BLOCK, SCAN_BLOCK), 0)
           >= lax.broadcasted_iota(jnp.int32, (SCAN_BLOCK, SCAN_BLOCK), 1))
    mid = SCAN_BLOCK // 2
    nt = (((1,), (1,)), ((), ()))
    tn = (((0,), (0,)), ((), ()))

    def head_consts(hh):
        lanes = slice(hh * LANE, (hh + 1) * LANE)
        logits = lbl_ref[:, lanes]
        ex = jnp.exp(logits - jnp.max(logits, axis=0, keepdims=True))
        lb = jnp.sum(ex[0:layer + 1, :], axis=0, keepdims=True) / jnp.sum(ex, axis=0, keepdims=True)
        return lb, 1.0 - lb, convw_ref[0:1, lanes], convw_ref[1:2, lanes], convw_ref[2:3, lanes]

    consts = [head_consts(hh) for hh in range(hb)]

    def prefix8(x):
        sh = 1
        while sh < SUBLANES:
            x = x + jnp.where(sub >= sh, pltpu.roll(x, sh, 0), 0.0)
            sh *= 2
        return x

    def conv_block(hh, lo):
        _, _, w0, w1, w2 = consts[hh]
        rows = pl.ds(lo, SCAN_BLOCK)
        cu = cg_ref[hh, rows, :] * u_ref[hh, rows, :]
        cu_ref[hh, pl.ds(SUBLANES + lo, SCAN_BLOCK), :] = cu
        s1 = cu_ref[hh, pl.ds(SUBLANES - 1 + lo, SCAN_BLOCK), :]
        s2 = cu_ref[hh, pl.ds(SUBLANES - 2 + lo, SCAN_BLOCK), :]
        conv = w0 * s2 + w1 * s1 + w2 * cu
        ya_ref[rows, hh * LANE:(hh + 1) * LANE] = (bg_ref[hh, rows, :] * conv).astype(ya_ref.dtype)

    def state_free(hh, lo):
        lb, one_m_lb = consts[hh][0:2]
        rows = pl.ds(lo, SCAN_BLOCK)
        f = lb + one_m_lb * jax.nn.sigmoid(fz_ref[hh, rows, :])
        g2 = jnp.log2(f)
        kj = 1.0 - f
        qj = _silu(q_ref[hh, rows, :])
        vj = iv_ref[hh, rows, :].astype(BF16)
        parts = []
        total = None
        for p in range(SCAN_BLOCK // SUBLANES):
            part = prefix8(g2[p * SUBLANES:(p + 1) * SUBLANES, :])
            if total is not None:
                part = part + total
            total = part[SUBLANES - 1:SUBLANES, :]
            parts.append(part)
        lcj = jnp.concatenate(parts, axis=0)
        ref = lcj[mid - 1:mid, :]
        llast = total
        qd = (qj * jnp.exp2(lcj - ref)).astype(BF16)
        kd = (kj * jnp.exp2(ref - lcj)).astype(BF16)
        att = lax.dot_general(qd, kd, nt, preferred_element_type=F32)
        att = jnp.where(tri, att, 0.0).astype(BF16)
        o_intra = jnp.dot(att, vj, preferred_element_type=F32)
        kdec = (kj * jnp.exp2(llast - lcj)).astype(BF16)
        upd = lax.dot_general(vj, kdec, tn, preferred_element_type=F32)
        return o_intra, upd, jnp.exp2(llast), (qj * jnp.exp2(lcj)).astype(BF16)

    blocks = [(hh, j * SCAN_BLOCK) for hh in range(hb) for j in range(nsb)]
    st = [st_ref[hh] for hh in range(hb)]
    pending = {}
    for n in range(len(blocks) + STATE_LAG):
        if n < len(blocks):
            conv_block(*blocks[n])
            pending[n] = state_free(*blocks[n])
        if n >= STATE_LAG:
            hh, lo = blocks[n - STATE_LAG]
            o_intra, upd, decay, qe = pending.pop(n - STATE_LAG)
            o = o_intra + lax.dot_general(qe, st[hh].astype(BF16), nt, preferred_element_type=F32)
            st[hh] = st[hh] * decay + upd
            rows = pl.ds(lo, SCAN_BLOCK)
            ms = jnp.mean(o * o, axis=-1, keepdims=True)
            yb = o * lax.rsqrt(ms + NORM_EPS) * (hgn * _silu(og_ref[hh, rows, :]))
            yb_ref[rows, hh * LANE:(hh + 1) * LANE] = yb.astype(yb_ref.dtype)
    for hh in range(hb):
        st_ref[hh] = st[hh]
        tail = cu_ref[hh, SUBLANES + tc - (CONV_W - 1):SUBLANES + tc, :]
        cu_ref[hh, SUBLANES - (CONV_W - 1):SUBLANES, :] = tail
        nconv_ref[0, :, hh * LANE:(hh + 1) * LANE] = tail

    @pl.when(c == n_chunks - 1)
    def _fin():
        for hh in range(hb):
            nst_ref[0, hh] = st[hh].T


def _mixer(proj, conv_w, lb_logits, hg_norm, *, row0, nseq, t, tc, hb, n_groups, layer, cache=None, state=None):
    has_cache = cache is not None
    if has_cache != (state is not None):
        raise ValueError("conv cache and recurrent state come together")
    n_chunks = t // tc
    rb0 = row0 // tc
    n_hblk = n_groups // hb

    def pspec(seg):
        return pl.BlockSpec((hb, tc, LANE), lambda b, h, c: (seg * n_hblk + h, rb0 + b * n_chunks + c, 0))

    in_specs = [pspec(s) for s in range(7)]
    in_specs += [
        pl.BlockSpec((CONV_W, hb * LANE), lambda b, h, c: (0, h)),
        pl.BlockSpec((lb_logits.shape[0], hb * LANE), lambda b, h, c: (0, h)),
        pl.BlockSpec((1, LANE), lambda b, h, c: (0, 0)),
    ]
    args = [proj] * 7 + [conv_w, lb_logits, hg_norm]
    if has_cache:
        in_specs += [
            pl.BlockSpec((1, CONV_W - 1, hb * LANE), lambda b, h, c: (b, 0, h)),
            pl.BlockSpec((1, hb, HEAD_DIM, HEAD_DIM), lambda b, h, c: (b, h, 0, 0)),
        ]
        args += [cache, state]
    d_half = n_groups * LANE
    yspec = pl.BlockSpec((tc, hb * LANE), lambda b, h, c: (b * n_chunks + c, h))
    out_specs = [
        yspec, yspec,
        pl.BlockSpec((1, CONV_W - 1, hb * LANE), lambda b, h, c: (b, 0, h)),
        pl.BlockSpec((1, hb, HEAD_DIM, HEAD_DIM), lambda b, h, c: (b, h, 0, 0)),
    ]
    out_shape = [
        jax.ShapeDtypeStruct((nseq * t, d_half), BF16),
        jax.ShapeDtypeStruct((nseq * t, d_half), BF16),
        jax.ShapeDtypeStruct((nseq, CONV_W - 1, d_half), F32),
        jax.ShapeDtypeStruct((nseq, n_groups, HEAD_DIM, HEAD_DIM), F32),
    ]
    kern = functools.partial(_mixer_kernel, tc=tc, hb=hb, n_chunks=n_chunks, has_cache=has_cache,
                             layer=layer)
    return pl.pallas_call(
        kern,
        grid=(nseq, n_hblk, n_chunks),
        in_specs=in_specs,
        out_specs=out_specs,
        out_shape=out_shape,
        scratch_shapes=[pltpu.VMEM((hb, HEAD_DIM, HEAD_DIM), F32),
                        pltpu.VMEM((hb, SUBLANES + tc, LANE), F32)],
        compiler_params=_params(("parallel", "parallel", "arbitrary")),
        name="mixer_cached" if has_cache else "mixer_prompt",
    )(*args)


def _out_proj_kernel(yap_ref, ybp_ref, yas_ref, ybs_ref, w_ref, xp_ref, xs_ref, o_ref, *, n_p):
    i = pl.program_id(1)

    def emit(ya_ref, yb_ref, x_ref):
        y = jnp.concatenate([ya_ref[...], yb_ref[...]], axis=1)
        o_ref[...] = x_ref[...] + jnp.dot(y, w_ref[...], preferred_element_type=F32)

    @pl.when(i < n_p)
    def _():
        emit(yap_ref, ybp_ref, xp_ref)

    @pl.when(i >= n_p)
    def _():
        emit(yas_ref, ybs_ref, xs_ref)


def _out_proj(ya_p, yb_p, ya_s, yb_s, w, xp, xs, tm, tn):
    mp, dh = ya_p.shape
    ms = ya_s.shape[0]
    kdim, n = w.shape
    n_p, n_s = mp // tm, ms // tm

    def prow(j, i):
        return jnp.minimum(i, n_p - 1)

    def srow(j, i):
        return jnp.maximum(i - n_p, 0)

    return pl.pallas_call(
        functools.partial(_out_proj_kernel, n_p=n_p),
        grid=(n // tn, n_p + n_s),
        in_specs=[
            pl.BlockSpec((tm, dh), lambda j, i: (prow(j, i), 0)),
            pl.BlockSpec((tm, dh), lambda j, i: (prow(j, i), 0)),
            pl.BlockSpec((tm, dh), lambda j, i: (srow(j, i), 0)),
            pl.BlockSpec((tm, dh), lambda j, i: (srow(j, i), 0)),
            pl.BlockSpec((kdim, tn), lambda j, i: (0, j)),
            pl.BlockSpec((tm, tn), lambda j, i: (prow(j, i), j)),
            pl.BlockSpec((tm, tn), lambda j, i: (srow(j, i), j)),
        ],
        out_specs=pl.BlockSpec((tm, tn), lambda j, i: (i, j)),
        out_shape=jax.ShapeDtypeStruct((mp + ms, n), F32),
        compiler_params=_params(("parallel", "parallel")),
        name="out_proj",
    )(ya_p, yb_p, ya_s, yb_s, w, xp, xs)


def _ffn_up_kernel(a_ref, wg_ref, wu_ref, o_ref, *, n_valid):
    j = pl.program_id(0)

    @pl.when(j < n_valid)
    def _():
        a = a_ref[...]
        g = jnp.dot(a, wg_ref[...].astype(BF16), preferred_element_type=F32)
        u = jnp.dot(a, wu_ref[...].astype(BF16), preferred_element_type=F32)
        o_ref[...] = (_silu(g) * u).astype(o_ref.dtype)

    @pl.when(j >= n_valid)
    def _():
        o_ref[...] = jnp.zeros_like(o_ref)


def _ffn_up(h, wg, wu, tm, tn, f_pad):
    m, k = h.shape
    n_valid = wg.shape[1] // tn
    wspec = pl.BlockSpec((k, tn), lambda j, i: (0, jnp.minimum(j, n_valid - 1)))
    return pl.pallas_call(
        functools.partial(_ffn_up_kernel, n_valid=n_valid),
        grid=(f_pad // tn, m // tm),
        in_specs=[pl.BlockSpec((tm, k), lambda j, i: (i, 0)), wspec, wspec],
        out_specs=pl.BlockSpec((tm, tn), lambda j, i: (i, j)),
        out_shape=jax.ShapeDtypeStruct((m, f_pad), BF16),
        compiler_params=_params(("parallel", "parallel")),
        name="ffn_up",
    )(h, wg, wu)


def _ffn_down_kernel(a_ref, w_ref, x_ref, o_ref, *, k_valid):
    rows = lax.broadcasted_iota(jnp.int32, w_ref.shape, 0)
    w = jnp.where(rows < k_valid, w_ref[...], jnp.zeros_like(w_ref))
    o_ref[...] = x_ref[...] + jnp.dot(a_ref[...], w, preferred_element_type=F32)


def _ffn_down(act, w, x1, tm, tn):
    m, kf = act.shape
    n = w.shape[1]
    return pl.pallas_call(
        functools.partial(_ffn_down_kernel, k_valid=w.shape[0]),
        grid=(n // tn, m // tm),
        in_specs=[
            pl.BlockSpec((tm, kf), lambda j, i: (i, 0)),
            pl.BlockSpec((kf, tn), lambda j, i: (0, j)),
            pl.BlockSpec((tm, tn), lambda j, i: (i, j)),
        ],
        out_specs=pl.BlockSpec((tm, tn), lambda j, i: (i, j)),
        out_shape=jax.ShapeDtypeStruct((m, n), F32),
        compiler_params=_params(("parallel", "parallel")),
        name="ffn_down",
    )(act, w, x1)


def _mixer_tiles(t, n_groups):
    tc = _largest_tile(t, MIXER_BLOCKS_PER_STEP * SCAN_BLOCK, SCAN_BLOCK)
    hb = 1
    while hb < n_groups and 2 * hb * tc <= MIXER_BLOCKS_PER_STEP * SCAN_BLOCK and n_groups % (2 * hb) == 0:
        hb *= 2
    return dict(tc=tc, hb=hb)


def _tiles(mp, ms, tp, ts, n_groups, d, n_in, d_ff, f_pad):
    m = mp + ms
    tr = _largest_tile(_gcd(mp, ms), 512, BF16_ROWS)
    return dict(
        tr=tr,
        tm=_largest_tile(m, 1536, BF16_ROWS),
        tn_in=_largest_tile(n_in, 512, LANE),
        tn_out=_largest_tile(d, 1024, LANE),
        tn_up=_largest_tile(_gcd(d_ff, f_pad), 256, LANE),
        tm_down=_largest_tile(m, 528, BF16_ROWS),
        tn_down=_largest_tile(d, 512, LANE),
        mix_p=_mixer_tiles(tp, n_groups),
        mix_s=_mixer_tiles(ts, n_groups),
    )


def _gcd(a, b):
    while b:
        a, b = b, a % b
    return a


def kernel(x_prompt, x_sample, cache_conv, state_hgrn, norm_mix, w_in, conv_w, lb_logits, hg_norm,
           w_out, norm_ffn, w_gate, w_up, w_down, norm_final):
    bp, tp, d = x_prompt.shape
    bs, ts, _ = x_sample.shape
    depth = w_in.shape[0]
    if depth != 1:
        raise NotImplementedError("one layer")
    layer = 0
    n_in = w_in.shape[2]
    d_ff = w_gate.shape[2]
    d_half = conv_w.shape[2]
    n_groups = d_half // LANE
    if n_in != 7 * d_half or state_hgrn.shape[2] != n_groups or w_out.shape[1] != 2 * d_half:
        raise ValueError("unexpected layer geometry")
    mp, ms = bp * tp, bs * ts
    m = mp + ms
    f_pad = _round_up(d_ff, 1024)
    t = _tiles(mp, ms, tp, ts, n_groups, d, n_in, d_ff, f_pad)

    xp = x_prompt.reshape(mp, d)
    xs = x_sample.reshape(ms, d)
    w_out_b = w_out[layer].astype(BF16)
    w_down_b = w_down[layer].astype(BF16)

    h = _norm_two_src(xp, xs, norm_mix[layer].reshape(1, d), t["tr"])
    proj = _in_proj(h, w_in[layer], t["tm"], t["tn_in"])

    mix_args = dict(n_groups=n_groups, layer=layer)
    hgn = hg_norm[layer].reshape(1, HEAD_DIM)
    ya_p, yb_p, conv_p, hg_p = _mixer(proj, conv_w[layer], lb_logits, hgn, row0=0, nseq=bp, t=tp,
                                      **t["mix_p"], **mix_args)
    ya_s, yb_s, conv_s, hg_s = _mixer(proj, conv_w[layer], lb_logits, hgn, row0=mp, nseq=bs, t=ts,
                                      cache=cache_conv[layer], state=state_hgrn[layer],
                                      **t["mix_s"], **mix_args)

    x1 = _out_proj(ya_p, yb_p, ya_s, yb_s, w_out_b, xp, xs, t["tr"], t["tn_out"])
    h2 = _norm(x1, norm_ffn[layer].reshape(1, d), t["tr"])
    act = _ffn_up(h2, w_gate[layer], w_up[layer], t["tm"], t["tn_up"], f_pad)
    x2 = _ffn_down(act, w_down_b, x1, t["tm_down"], t["tn_down"])
    y_p, y_s = _norm_two_sink(x2, norm_final.reshape(1, d), mp, t["tr"])

    return (y_p.reshape(bp, tp, d), y_s.reshape(bs, ts, d),
            conv_p[None], hg_p[None].astype(x_prompt.dtype),
            conv_s[None].astype(cache_conv.dtype), hg_s[None].astype(state_hgrn.dtype))
```

```python
import functools

import jax
import jax.numpy as jnp
from jax import lax
from jax.experimental import pallas as pl
from jax.experimental.pallas import tpu as pltpu

LANE = 128
SUBLANES = 8
BF16_ROWS = 16
SCAN_BLOCK = 16
HEAD_DIM = 128
CONV_W = 3
NORM_EPS = 1e-6
V7X_VMEM_BYTES = 64 * 1024 * 1024
VMEM_LIMIT = V7X_VMEM_BYTES - 6 * 1024 * 1024

F32 = jnp.float32
BF16 = jnp.bfloat16


def _largest_tile(n, cap, quantum):
    best = None
    for t in range(quantum, min(n, cap) + 1, quantum):
        if n % t == 0:
            best = t
    if best is None:
        raise ValueError(f"no tile for n={n} cap={cap} quantum={quantum}")
    return best


def _round_up(n, m):
    return (n + m - 1) // m * m


def _params(sem):
    return pltpu.CompilerParams(dimension_semantics=sem, vmem_limit_bytes=VMEM_LIMIT)


def _rms_rows(x, w):
    ms = jnp.mean(x * x, axis=-1, keepdims=True)
    return x * lax.rsqrt(ms + NORM_EPS) * w


def _silu(x):
    return x * jax.nn.sigmoid(x)


def _norm_two_src_kernel(xp_ref, xs_ref, w_ref, o_ref, *, n_p):
    i = pl.program_id(0)

    @pl.when(i < n_p)
    def _():
        o_ref[...] = _rms_rows(xp_ref[...], w_ref[...]).astype(o_ref.dtype)

    @pl.when(i >= n_p)
    def _():
        o_ref[...] = _rms_rows(xs_ref[...], w_ref[...]).astype(o_ref.dtype)


def _norm_two_src(xp, xs, w, tr):
    mp, d = xp.shape
    ms = xs.shape[0]
    n_p, n_s = mp // tr, ms // tr
    return pl.pallas_call(
        functools.partial(_norm_two_src_kernel, n_p=n_p),
        grid=(n_p + n_s,),
        in_specs=[
            pl.BlockSpec((tr, d), lambda i: (jnp.minimum(i, n_p - 1), 0)),
            pl.BlockSpec((tr, d), lambda i: (jnp.maximum(i - n_p, 0), 0)),
            pl.BlockSpec((1, d), lambda i: (0, 0)),
        ],
        out_specs=pl.BlockSpec((tr, d), lambda i: (i, 0)),
        out_shape=jax.ShapeDtypeStruct((mp + ms, d), BF16),
        compiler_params=_params(("parallel",)),
        name="norm1",
    )(xp, xs, w)


def _norm_kernel(x_ref, w_ref, o_ref):
    o_ref[...] = _rms_rows(x_ref[...], w_ref[...]).astype(o_ref.dtype)


def _norm(x, w, tr):
    m, d = x.shape
    return pl.pallas_call(
        _norm_kernel,
        grid=(m // tr,),
        in_specs=[pl.BlockSpec((tr, d), lambda i: (i, 0)), pl.BlockSpec((1, d), lambda i: (0, 0))],
        out_specs=pl.BlockSpec((tr, d), lambda i: (i, 0)),
        out_shape=jax.ShapeDtypeStruct((m, d), BF16),
        compiler_params=_params(("parallel",)),
        name="norm2",
    )(x, w)


def _norm_two_sink_kernel(x_ref, w_ref, op_ref, os_ref, *, n_p):
    i = pl.program_id(0)

    @pl.when(i < n_p)
    def _():
        op_ref[...] = _rms_rows(x_ref[...], w_ref[...])

    @pl.when(i >= n_p)
    def _():
        os_ref[...] = _rms_rows(x_ref[...], w_ref[...])


def _norm_two_sink(x, w, mp, tr):
    m, d = x.shape
    ms = m - mp
    n_p, n_s = mp // tr, ms // tr
    return pl.pallas_call(
        functools.partial(_norm_two_sink_kernel, n_p=n_p),
        grid=(n_p + n_s,),
        in_specs=[pl.BlockSpec((tr, d), lambda i: (i, 0)), pl.BlockSpec((1, d), lambda i: (0, 0))],
        out_specs=[
            pl.BlockSpec((tr, d), lambda i: (jnp.minimum(i, n_p - 1), 0)),
            pl.BlockSpec((tr, d), lambda i: (jnp.maximum(i - n_p, 0), 0)),
        ],
        out_shape=[jax.ShapeDtypeStruct((mp, d), F32), jax.ShapeDtypeStruct((ms, d), F32)],
        compiler_params=_params(("arbitrary",)),
        name="final_norm",
    )(x, w)


def _in_proj_kernel(a_ref, b_ref, o_ref):
    acc = jnp.dot(a_ref[...], b_ref[...].astype(BF16), preferred_element_type=F32)
    for g in range(o_ref.shape[0]):
        o_ref[g] = acc[:, g * LANE:(g + 1) * LANE]


def _in_proj(h, w, tm, tn):
    m, k = h.shape
    n = w.shape[1]
    return pl.pallas_call(
        _in_proj_kernel,
        grid=(m // tm, n // tn),
        in_specs=[pl.BlockSpec((tm, k), lambda i, j: (i, 0)), pl.BlockSpec((k, tn), lambda i, j: (0, j))],
        out_specs=pl.BlockSpec((tn // LANE, tm, LANE), lambda i, j: (j, i, 0)),
        out_shape=jax.ShapeDtypeStruct((n // LANE, m, LANE), F32),
        compiler_params=_params(("parallel", "parallel")),
        name="in_proj",
    )(h, w)


STATE_LAG = 4
MIXER_BLOCKS_PER_STEP = 256


def _mixer_kernel(*refs, tc, hb, n_chunks, has_cache, layer):
    if has_cache:
        (bg_ref, cg_ref, u_ref, q_ref, fz_ref, iv_ref, og_ref, convw_ref, lbl_ref, hgn_ref,
         cbuf_ref, s0_ref, ya_ref, yb_ref, nconv_ref, nst_ref, st_ref, cu_ref) = refs
    else:
        (bg_ref, cg_ref, u_ref, q_ref, fz_ref, iv_ref, og_ref, convw_ref, lbl_ref, hgn_ref,
         ya_ref, yb_ref, nconv_ref, nst_ref, st_ref, cu_ref) = refs
    c = pl.program_id(2)
    nsb = tc // SCAN_BLOCK

    @pl.when(c == 0)
    def _init():
        for hh in range(hb):
            if has_cache:
                st_ref[hh] = s0_ref[0, hh].T
                cu_ref[hh, SUBLANES - (CONV_W - 1):SUBLANES, :] = cbuf_ref[0, :, hh * LANE:(hh + 1) * LANE]
            else:
                st_ref[hh] = jnp.zeros((HEAD_DIM, HEAD_DIM), F32)
                cu_ref[hh, 0:SUBLANES, :] = jnp.zeros((SUBLANES, LANE), F32)

    hgn = hgn_ref[...]
    sub = lax.broadcasted_iota(jnp.int32, (SUBLANES, LANE), 0)
    tri = (lax.broadcasted_iota(jnp.int32, (SCAN_BLOCK, SCAN_BLOCK), 0)
           >= lax.broadcasted_iota(jnp.int32, (SCAN_BLOCK, SCAN_BLOCK), 1))
    mid = SCAN_BLOCK // 2
    nt = (((1,), (1,)), ((), ()))
    tn = (((0,), (0,)), ((), ()))

    def head_consts(hh):
        lanes = slice(hh * LANE, (hh + 1) * LANE)
        logits = lbl_ref[:, lanes]
        ex = jnp.exp(logits - jnp.max(logits, axis=0, keepdims=True))
        lb = jnp.sum(ex[0:layer + 1, :], axis=0, keepdims=True) / jnp.sum(ex, axis=0, keepdims=True)
        return lb, 1.0 - lb, convw_ref[0:1, lanes], convw_ref[1:2, lanes], convw_ref[2:3, lanes]

    consts = [head_consts(hh) for hh in range(hb)]

    def prefix8(x):
        sh = 1
        while sh < SUBLANES:
            x = x + jnp.where(sub >= sh, pltpu.roll(x, sh, 0), 0.0)
            sh *= 2
        return x

    def conv_block(hh, lo):
        _, _, w0, w1, w2 = consts[hh]
        rows = pl.ds(lo, SCAN_BLOCK)
        cu = cg_ref[hh, rows, :] * u_ref[hh, rows, :]
        cu_ref[hh, pl.ds(SUBLANES + lo, SCAN_BLOCK), :] = cu
        s1 = cu_ref[hh, pl.ds(SUBLANES - 1 + lo, SCAN_BLOCK), :]
        s2 = cu_ref[hh, pl.ds(SUBLANES - 2 + lo, SCAN_BLOCK), :]
        conv = w0 * s2 + w1 * s1 + w2 * cu
        ya_ref[rows, hh * LANE:(hh + 1) * LANE] = (bg_ref[hh, rows, :] * conv).astype(ya_ref.dtype)

    def state_free(hh, lo):
        lb, one_m_lb = consts[hh][0:2]
        rows = pl.ds(lo, SCAN_BLOCK)
        f = lb + one_m_lb * jax.nn.sigmoid(fz_ref[hh, rows, :])
        g2 = jnp.log2(f)
        kj = 1.0 - f
        qj = _silu(q_ref[hh, rows, :])
        vj = iv_ref[hh, rows, :].astype(BF16)
        parts = []
        total = None
        for p in range(SCAN_BLOCK // SUBLANES):
            part = prefix8(g2[p * SUBLANES:(p + 1) * SUBLANES, :])
            if total is not None:
                part = part + total
            total = part[SUBLANES - 1:SUBLANES, :]
            parts.append(part)
        lcj = jnp.concatenate(parts, axis=0)
        ref = lcj[mid - 1:mid, :]
        llast = total
        qd = (qj * jnp.exp2(lcj - ref)).astype(BF16)
        kd = (kj * jnp.exp2(ref - lcj)).astype(BF16)
        att = lax.dot_general(qd, kd, nt, preferred_element_type=F32)
        att = jnp.where(tri, att, 0.0).astype(BF16)
        o_intra = jnp.dot(att, vj, preferred_element_type=F32)
        kdec = (kj * jnp.exp2(llast - lcj)).astype(BF16)
        upd = lax.dot_general(vj, kdec, tn, preferred_element_type=F32)
        return o_intra, upd, jnp.exp2(llast), (qj * jnp.exp2(lcj)).astype(BF16)

    blocks = [(hh, j * SCAN_BLOCK) for hh in range(hb) for j in range(nsb)]
    st = [st_ref[hh] for hh in range(hb)]
    pending = {}
    for n in range(len(blocks) + STATE_LAG):
        if n < len(blocks):
            conv_block(*blocks[n])
            pending[n] = state_free(*blocks[n])
        if n >= STATE_LAG:
            hh, lo = blocks[n - STATE_LAG]
            o_intra, upd, decay, qe = pending.pop(n - STATE_LAG)
            o = o_intra + lax.dot_general(qe, st[hh].astype(BF16), nt, preferred_element_type=F32)
            st[hh] = st[hh] * decay + upd
            rows = pl.ds(lo, SCAN_BLOCK)
            ms = jnp.mean(o * o, axis=-1, keepdims=True)
            yb = o * lax.rsqrt(ms + NORM_EPS) * (hgn * _silu(og_ref[hh, rows, :]))
            yb_ref[rows, hh * LANE:(hh + 1) * LANE] = yb.astype(yb_ref.dtype)
    for hh in range(hb):
        st_ref[hh] = st[hh]
        tail = cu_ref[hh, SUBLANES + tc - (CONV_W - 1):SUBLANES + tc, :]
        cu_ref[hh, SUBLANES - (CONV_W - 1):SUBLANES, :] = tail
        nconv_ref[0, :, hh * LANE:(hh + 1) * LANE] = tail

    @pl.when(c == n_chunks - 1)
    def _fin():
        for hh in range(hb):
            nst_ref[0, hh] = st[hh].T


def _mixer(proj, conv_w, lb_logits, hg_norm, *, row0, nseq, t, tc, hb, n_groups, layer, cache=None, state=None):
    has_cache = cache is not None
    if has_cache != (state is not None):
        raise ValueError("conv cache and recurrent state come together")
    n_chunks = t // tc
    rb0 = row0 // tc
    n_hblk = n_groups // hb

    def pspec(seg):
        return pl.BlockSpec((hb, tc, LANE), lambda b, h, c: (seg * n_hblk + h, rb0 + b * n_chunks + c, 0))

    in_specs = [pspec(s) for s in range(7)]
    in_specs += [
        pl.BlockSpec((CONV_W, hb * LANE), lambda b, h, c: (0, h)),
        pl.BlockSpec((lb_logits.shape[0], hb * LANE), lambda b, h, c: (0, h)),
        pl.BlockSpec((1, LANE), lambda b, h, c: (0, 0)),
    ]
    args = [proj] * 7 + [conv_w, lb_logits, hg_norm]
    if has_cache:
        in_specs += [
            pl.BlockSpec((1, CONV_W - 1, hb * LANE), lambda b, h, c: (b, 0, h)),
            pl.BlockSpec((1, hb, HEAD_DIM, HEAD_DIM), lambda b, h, c: (b, h, 0, 0)),
        ]
        args += [cache, state]
    d_half = n_groups * LANE
    yspec = pl.BlockSpec((tc, hb * LANE), lambda b, h, c: (b * n_chunks + c, h))
    out_specs = [
        yspec, yspec,
        pl.BlockSpec((1, CONV_W - 1, hb * LANE), lambda b, h, c: (b, 0, h)),
        pl.BlockSpec((1, hb, HEAD_DIM, HEAD_DIM), lambda b, h, c: (b, h, 0, 0)),
    ]
    out_shape = [
        jax.ShapeDtypeStruct((nseq * t, d_half), BF16),
        jax.ShapeDtypeStruct((nseq * t, d_half), BF16),
        jax.ShapeDtypeStruct((nseq, CONV_W - 1, d_half), F32),
        jax.ShapeDtypeStruct((nseq, n_groups, HEAD_DIM, HEAD_DIM), F32),
    ]
    kern = functools.partial(_mixer_kernel, tc=tc, hb=hb, n_chunks=n_chunks, has_cache=has_cache,
                             layer=layer)
    return pl.pallas_call(
        kern,
        grid=(nseq, n_hblk, n_chunks),
        in_specs=in_specs,
        out_specs=out_specs,
        out_shape=out_shape,
        scratch_shapes=[pltpu.VMEM((hb, HEAD_DIM, HEAD_DIM), F32),
                        pltpu.VMEM((hb, SUBLANES + tc, LANE), F32)],
        compiler_params=_params(("parallel", "parallel", "arbitrary")),
        name="mixer_cached" if has_cache else "mixer_prompt",
    )(*args)


def _out_proj_kernel(yap_ref, ybp_ref, yas_ref, ybs_ref, w_ref, xp_ref, xs_ref, o_ref, *, n_p):
    i = pl.program_id(1)

    def emit(ya_ref, yb_ref, x_ref):
        y = jnp.concatenate([ya_ref[...], yb_ref[...]], axis=1)
        o_ref[...] = x_ref[...] + jnp.dot(y, w_ref[...], preferred_element_type=F32)

    @pl.when(i < n_p)
    def _():
        emit(yap_ref, ybp_ref, xp_ref)

    @pl.when(i >= n_p)
    def _():
        emit(yas_ref, ybs_ref, xs_ref)


def _out_proj(ya_p, yb_p, ya_s, yb_s, w, xp, xs, tm, tn):
    mp, dh = ya_p.shape
    ms = ya_s.shape[0]
    kdim, n = w.shape
    n_p, n_s = mp // tm, ms // tm

    def prow(j, i):
        return jnp.minimum(i, n_p - 1)

    def srow(j, i):
        return jnp.maximum(i - n_p, 0)

    return pl.pallas_call(
        functools.partial(_out_proj_kernel, n_p=n_p),
        grid=(n // tn, n_p + n_s),
        in_specs=[
            pl.BlockSpec((tm, dh), lambda j, i: (prow(j, i), 0)),
            pl.BlockSpec((tm, dh), lambda j, i: (prow(j, i), 0)),
            pl.BlockSpec((tm, dh), lambda j, i: (srow(j, i), 0)),
            pl.BlockSpec((tm, dh), lambda j, i: (srow(j, i), 0)),
            pl.BlockSpec((kdim, tn), lambda j, i: (0, j)),
            pl.BlockSpec((tm, tn), lambda j, i: (prow(j, i), j)),
            pl.BlockSpec((tm, tn), lambda j, i: (srow(j, i), j)),
        ],
        out_specs=pl.BlockSpec((tm, tn), lambda j, i: (i, j)),
        out_shape=jax.ShapeDtypeStruct((mp + ms, n), F32),
        compiler_params=_params(("parallel", "parallel")),
        name="out_proj",
    )(ya_p, yb_p, ya_s, yb_s, w, xp, xs)


def _ffn_up_kernel(a_ref, wg_ref, wu_ref, o_ref, wg_bf, wu_bf, *, n_valid):
    j = pl.program_id(0)

    @pl.when(jnp.logical_and(j < n_valid, pl.program_id(1) == 0))
    def _():
        wg_bf[...] = wg_ref[...].astype(BF16)
        wu_bf[...] = wu_ref[...].astype(BF16)

    @pl.when(j < n_valid)
    def _():
        a = a_ref[...]
        g = jnp.dot(a, wg_bf[...], preferred_element_type=F32)
        u = jnp.dot(a, wu_bf[...], preferred_element_type=F32)
        o_ref[...] = (_silu(g) * u).astype(o_ref.dtype)

    @pl.when(j >= n_valid)
    def _():
        o_ref[...] = jnp.zeros_like(o_ref)


def _ffn_up(h, wg, wu, tm, tn, f_pad):
    m, k = h.shape
    n_valid = wg.shape[1] // tn
    wspec = pl.BlockSpec((k, tn), lambda j, i: (0, jnp.minimum(j, n_valid - 1)))
    return pl.pallas_call(
        functools.partial(_ffn_up_kernel, n_valid=n_valid),
        grid=(f_pad // tn, m // tm),
        in_specs=[pl.BlockSpec((tm, k), lambda j, i: (i, 0)), wspec, wspec],
        out_specs=pl.BlockSpec((tm, tn), lambda j, i: (i, j)),
        out_shape=jax.ShapeDtypeStruct((m, f_pad), BF16),
        scratch_shapes=[pltpu.VMEM((k, tn), BF16), pltpu.VMEM((k, tn), BF16)],
        compiler_params=_params(("parallel", "arbitrary")),
        name="ffn_up",
    )(h, wg, wu)


def _ffn_down_kernel(a_ref, w_ref, x_ref, o_ref, *, k_valid):
    rows = lax.broadcasted_iota(jnp.int32, w_ref.shape, 0)
    w = jnp.where(rows < k_valid, w_ref[...], jnp.zeros_like(w_ref))
    o_ref[...] = x_ref[...] + jnp.dot(a_ref[...], w, preferred_element_type=F32)


def _ffn_down(act, w, x1, tm, tn):
    m, kf = act.shape
    n = w.shape[1]
    return pl.pallas_call(
        functools.partial(_ffn_down_kernel, k_valid=w.shape[0]),
        grid=(n // tn, m // tm),
        in_specs=[
            pl.BlockSpec((tm, kf), lambda j, i: (i, 0)),
            pl.BlockSpec((kf, tn), lambda j, i: (0, j)),
            pl.BlockSpec((tm, tn), lambda j, i: (i, j)),
        ],
        out_specs=pl.BlockSpec((tm, tn), lambda j, i: (i, j)),
        out_shape=jax.ShapeDtypeStruct((m, n), F32),
        compiler_params=_params(("parallel", "parallel")),
        name="ffn_down",
    )(act, w, x1)


def _mixer_tiles(t, n_groups):
    tc = _largest_tile(t, MIXER_BLOCKS_PER_STEP * SCAN_BLOCK, SCAN_BLOCK)
    hb = 1
    while hb < n_groups and 2 * hb * tc <= MIXER_BLOCKS_PER_STEP * SCAN_BLOCK and n_groups % (2 * hb) == 0:
        hb *= 2
    return dict(tc=tc, hb=hb)


def _tiles(mp, ms, tp, ts, n_groups, d, n_in, d_ff, f_pad):
    m = mp + ms
    tr = _largest_tile(_gcd(mp, ms), 512, BF16_ROWS)
    return dict(
        tr=tr,
        tm=_largest_tile(m, 1536, BF16_ROWS),
        tn_in=_largest_tile(n_in, 512, LANE),
        tn_out=_largest_tile(d, 1024, LANE),
        tn_up=_largest_tile(_gcd(d_ff, f_pad), 256, LANE),
        tm_down=_largest_tile(m, 528, BF16_ROWS),
        tn_down=_largest_tile(d, 512, LANE),
        mix_p=_mixer_tiles(tp, n_groups),
        mix_s=_mixer_tiles(ts, n_groups),
    )


def _gcd(a, b):
    while b:
        a, b = b, a % b
    return a


def kernel(x_prompt, x_sample, cache_conv, state_hgrn, norm_mix, w_in, conv_w, lb_logits, hg_norm,
           w_out, norm_ffn, w_gate, w_up, w_down, norm_final):
    bp, tp, d = x_prompt.shape
    bs, ts, _ = x_sample.shape
    depth = w_in.shape[0]
    if depth != 1:
        raise NotImplementedError("one layer")
    layer = 0
    n_in = w_in.shape[2]
    d_ff = w_gate.shape[2]
    d_half = conv_w.shape[2]
    n_groups = d_half // LANE
    if n_in != 7 * d_half or state_hgrn.shape[2] != n_groups or w_out.shape[1] != 2 * d_half:
        raise ValueError("unexpected layer geometry")
    mp, ms = bp * tp, bs * ts
    m = mp + ms
    f_pad = _round_up(d_ff, 1024)
    t = _tiles(mp, ms, tp, ts, n_groups, d, n_in, d_ff, f_pad)

    xp = x_prompt.reshape(mp, d)
    xs = x_sample.reshape(ms, d)
    w_out_b = w_out[layer].astype(BF16)
    w_down_b = w_down[layer].astype(BF16)

    h = _norm_two_src(xp, xs, norm_mix[layer].reshape(1, d), t["tr"])
    proj = _in_proj(h, w_in[layer], t["tm"], t["tn_in"])

    mix_args = dict(n_groups=n_groups, layer=layer)
    hgn = hg_norm[layer].reshape(1, HEAD_DIM)
    ya_p, yb_p, conv_p, hg_p = _mixer(proj, conv_w[layer], lb_logits, hgn, row0=0, nseq=bp, t=tp,
                                      **t["mix_p"], **mix_args)
    ya_s, yb_s, conv_s, hg_s = _mixer(proj, conv_w[layer], lb_logits, hgn, row0=mp, nseq=bs, t=ts,
                                      cache=cache_conv[layer], state=state_hgrn[layer],
                                      **t["mix_s"], **mix_args)

    x1 = _out_proj(ya_p, yb_p, ya_s, yb_s, w_out_b, xp, xs, t["tr"], t["tn_out"])
    h2 = _norm(x1, norm_ffn[layer].reshape(1, d), t["tr"])
    act = _ffn_up(h2, w_gate[layer], w_up[layer], t["tm"], t["tn_up"], f_pad)
    x2 = _ffn_down(act, w_down_b, x1, t["tm_down"], t["tn_down"])
    y_p, y_s = _norm_two_sink(x2, norm_final.reshape(1, d), mp, t["tr"])

    return (y_p.reshape(bp, tp, d), y_s.reshape(bs, ts, d),
            conv_p[None], hg_p[None].astype(x_prompt.dtype),
            conv_s[None].astype(cache_conv.dtype), hg_s[None].astype(state_hgrn.dtype))
```

```python
import functools

import jax
import jax.numpy as jnp
from jax import lax
from jax.experimental import pallas as pl
from jax.experimental.pallas import tpu as pltpu

LANE = 128
SUBLANES = 8
BF16_ROWS = 16
SCAN_BLOCK = 16
HEAD_DIM = 128
CONV_W = 3
NORM_EPS = 1e-6
V7X_VMEM_BYTES = 64 * 1024 * 1024
VMEM_LIMIT = V7X_VMEM_BYTES - 6 * 1024 * 1024

F32 = jnp.float32
BF16 = jnp.bfloat16


def _largest_tile(n, cap, quantum):
    best = None
    for t in range(quantum, min(n, cap) + 1, quantum):
        if n % t == 0:
            best = t
    if best is None:
        raise ValueError(f"no tile for n={n} cap={cap} quantum={quantum}")
    return best


def _round_up(n, m):
    return (n + m - 1) // m * m


def _params(sem):
    return pltpu.CompilerParams(dimension_semantics=sem, vmem_limit_bytes=VMEM_LIMIT)


def _rms_rows(x, w):
    ms = jnp.mean(x * x, axis=-1, keepdims=True)
    return x * lax.rsqrt(ms + NORM_EPS) * w


def _silu(x):
    return x * jax.nn.sigmoid(x)


def _norm_two_src_kernel(xp_ref, xs_ref, w_ref, o_ref, *, n_p):
    i = pl.program_id(0)

    @pl.when(i < n_p)
    def _():
        o_ref[...] = _rms_rows(xp_ref[...], w_ref[...]).astype(o_ref.dtype)

    @pl.when(i >= n_p)
    def _():
        o_ref[...] = _rms_rows(xs_ref[...], w_ref[...]).astype(o_ref.dtype)


def _norm_two_src(xp, xs, w, tr):
    mp, d = xp.shape
    ms = xs.shape[0]
    n_p, n_s = mp // tr, ms // tr
    return pl.pallas_call(
        functools.partial(_norm_two_src_kernel, n_p=n_p),
        grid=(n_p + n_s,),
        in_specs=[
            pl.BlockSpec((tr, d), lambda i: (jnp.minimum(i, n_p - 1), 0)),
            pl.BlockSpec((tr, d), lambda i: (jnp.maximum(i - n_p, 0), 0)),
            pl.BlockSpec((1, d), lambda i: (0, 0)),
        ],
        out_specs=pl.BlockSpec((tr, d), lambda i: (i, 0)),
        out_shape=jax.ShapeDtypeStruct((mp + ms, d), BF16),
        compiler_params=_params(("parallel",)),
        name="norm1",
    )(xp, xs, w)


def _norm_two_sink_kernel(x_ref, w_ref, op_ref, os_ref, *, n_p):
    i = pl.program_id(0)

    @pl.when(i < n_p)
    def _():
        op_ref[...] = _rms_rows(x_ref[...], w_ref[...])

    @pl.when(i >= n_p)
    def _():
        os_ref[...] = _rms_rows(x_ref[...], w_ref[...])


def _norm_two_sink(x, w, mp, tr):
    m, d = x.shape
    ms = m - mp
    n_p, n_s = mp // tr, ms // tr
    return pl.pallas_call(
        functools.partial(_norm_two_sink_kernel, n_p=n_p),
        grid=(n_p + n_s,),
        in_specs=[pl.BlockSpec((tr, d), lambda i: (i, 0)), pl.BlockSpec((1, d), lambda i: (0, 0))],
        out_specs=[
            pl.BlockSpec((tr, d), lambda i: (jnp.minimum(i, n_p - 1), 0)),
            pl.BlockSpec((tr, d), lambda i: (jnp.maximum(i - n_p, 0), 0)),
        ],
        out_shape=[jax.ShapeDtypeStruct((mp, d), F32), jax.ShapeDtypeStruct((ms, d), F32)],
        compiler_params=_params(("arbitrary",)),
        name="final_norm",
    )(x, w)


def _in_proj_kernel(a_ref, b_ref, o_ref):
    acc = jnp.dot(a_ref[...], b_ref[...].astype(BF16), preferred_element_type=F32)
    for g in range(o_ref.shape[0]):
        o_ref[g] = acc[:, g * LANE:(g + 1) * LANE]


def _in_proj(h, w, tm, tn):
    m, k = h.shape
    n = w.shape[1]
    return pl.pallas_call(
        _in_proj_kernel,
        grid=(m // tm, n // tn),
        in_specs=[pl.BlockSpec((tm, k), lambda i, j: (i, 0)), pl.BlockSpec((k, tn), lambda i, j: (0, j))],
        out_specs=pl.BlockSpec((tn // LANE, tm, LANE), lambda i, j: (j, i, 0)),
        out_shape=jax.ShapeDtypeStruct((n // LANE, m, LANE), F32),
        compiler_params=_params(("parallel", "parallel")),
        name="in_proj",
    )(h, w)


STATE_LAG = 4
MIXER_BLOCKS_PER_STEP = 256


def _mixer_kernel(*refs, tc, hb, n_chunks, has_cache, layer):
    if has_cache:
        (bg_ref, cg_ref, u_ref, q_ref, fz_ref, iv_ref, og_ref, convw_ref, lbl_ref, hgn_ref,
         cbuf_ref, s0_ref, ya_ref, yb_ref, nconv_ref, nst_ref, st_ref, cu_ref) = refs
    else:
        (bg_ref, cg_ref, u_ref, q_ref, fz_ref, iv_ref, og_ref, convw_ref, lbl_ref, hgn_ref,
         ya_ref, yb_ref, nconv_ref, nst_ref, st_ref, cu_ref) = refs
    c = pl.program_id(2)
    nsb = tc // SCAN_BLOCK

    @pl.when(c == 0)
    def _init():
        for hh in range(hb):
            if has_cache:
                st_ref[hh] = s0_ref[0, hh].T
                cu_ref[hh, SUBLANES - (CONV_W - 1):SUBLANES, :] = cbuf_ref[0, :, hh * LANE:(hh + 1) * LANE]
            else:
                st_ref[hh] = jnp.zeros((HEAD_DIM, HEAD_DIM), F32)
                cu_ref[hh, 0:SUBLANES, :] = jnp.zeros((SUBLANES, LANE), F32)

    hgn = hgn_ref[...]
    sub = lax.broadcasted_iota(jnp.int32, (SUBLANES, LANE), 0)
    tri = (lax.broadcasted_iota(jnp.int32, (SCAN_BLOCK, SCAN_BLOCK), 0)
           >= lax.broadcasted_iota(jnp.int32, (SCAN_BLOCK, SCAN_BLOCK), 1))
    mid = SCAN_BLOCK // 2
    nt = (((1,), (1,)), ((), ()))
    tn = (((0,), (0,)), ((), ()))

    def head_consts(hh):
        lanes = slice(hh * LANE, (hh + 1) * LANE)
        logits = lbl_ref[:, lanes]
        ex = jnp.exp(logits - jnp.max(logits, axis=0, keepdims=True))
        lb = jnp.sum(ex[0:layer + 1, :], axis=0, keepdims=True) / jnp.sum(ex, axis=0, keepdims=True)
        return lb, 1.0 - lb, convw_ref[0:1, lanes], convw_ref[1:2, lanes], convw_ref[2:3, lanes]

    consts = [head_consts(hh) for hh in range(hb)]

    def prefix8(x):
        sh = 1
        while sh < SUBLANES:
            x = x + jnp.where(sub >= sh, pltpu.roll(x, sh, 0), 0.0)
            sh *= 2
        return x

    def conv_block(hh, lo):
        _, _, w0, w1, w2 = consts[hh]
        rows = pl.ds(lo, SCAN_BLOCK)
        cu = cg_ref[hh, rows, :] * u_ref[hh, rows, :]
        cu_ref[hh, pl.ds(SUBLANES + lo, SCAN_BLOCK), :] = cu
        s1 = cu_ref[hh, pl.ds(SUBLANES - 1 + lo, SCAN_BLOCK), :]
        s2 = cu_ref[hh, pl.ds(SUBLANES - 2 + lo, SCAN_BLOCK), :]
        conv = w0 * s2 + w1 * s1 + w2 * cu
        ya_ref[rows, hh * LANE:(hh + 1) * LANE] = (bg_ref[hh, rows, :] * conv).astype(ya_ref.dtype)

    def state_free(hh, lo):
        lb, one_m_lb = consts[hh][0:2]
        rows = pl.ds(lo, SCAN_BLOCK)
        f = lb + one_m_lb * jax.nn.sigmoid(fz_ref[hh, rows, :])
        g2 = jnp.log2(f)
        kj = 1.0 - f
        qj = _silu(q_ref[hh, rows, :])
        vj = iv_ref[hh, rows, :].astype(BF16)
        parts = []
        total = None
        for p in range(SCAN_BLOCK // SUBLANES):
            part = prefix8(g2[p * SUBLANES:(p + 1) * SUBLANES, :])
            if total is not None:
                part = part + total
            total = part[SUBLANES - 1:SUBLANES, :]
            parts.append(part)
        lcj = jnp.concatenate(parts, axis=0)
        ref = lcj[mid - 1:mid, :]
        llast = total
        qd = (qj * jnp.exp2(lcj - ref)).astype(BF16)
        kd = (kj * jnp.exp2(ref - lcj)).astype(BF16)
        att = lax.dot_general(qd, kd, nt, preferred_element_type=F32)
        att = jnp.where(tri, att, 0.0).astype(BF16)
        o_intra = jnp.dot(att, vj, preferred_element_type=F32)
        kdec = (kj * jnp.exp2(llast - lcj)).astype(BF16)
        upd = lax.dot_general(vj, kdec, tn, preferred_element_type=F32)
        return o_intra, upd, jnp.exp2(llast), (qj * jnp.exp2(lcj)).astype(BF16)

    blocks = [(hh, j * SCAN_BLOCK) for hh in range(hb) for j in range(nsb)]
    st = [st_ref[hh] for hh in range(hb)]
    pending = {}
    for n in range(len(blocks) + STATE_LAG):
        if n < len(blocks):
            conv_block(*blocks[n])
            pending[n] = state_free(*blocks[n])
        if n >= STATE_LAG:
            hh, lo = blocks[n - STATE_LAG]
            o_intra, upd, decay, qe = pending.pop(n - STATE_LAG)
            o = o_intra + lax.dot_general(qe, st[hh].astype(BF16), nt, preferred_element_type=F32)
            st[hh] = st[hh] * decay + upd
            rows = pl.ds(lo, SCAN_BLOCK)
            ms = jnp.mean(o * o, axis=-1, keepdims=True)
            yb = o * lax.rsqrt(ms + NORM_EPS) * (hgn * _silu(og_ref[hh, rows, :]))
            yb_ref[rows, hh * LANE:(hh + 1) * LANE] = yb.astype(yb_ref.dtype)
    for hh in range(hb):
        st_ref[hh] = st[hh]
        tail = cu_ref[hh, SUBLANES + tc - (CONV_W - 1):SUBLANES + tc, :]
        cu_ref[hh, SUBLANES - (CONV_W - 1):SUBLANES, :] = tail
        nconv_ref[0, :, hh * LANE:(hh + 1) * LANE] = tail

    @pl.when(c == n_chunks - 1)
    def _fin():
        for hh in range(hb):
            nst_ref[0, hh] = st[hh].T


def _mixer(proj, conv_w, lb_logits, hg_norm, *, row0, nseq, t, tc, hb, n_groups, layer, cache=None, state=None):
    has_cache = cache is not None
    if has_cache != (state is not None):
        raise ValueError("conv cache and recurrent state come together")
    n_chunks = t // tc
    rb0 = row0 // tc
    n_hblk = n_groups // hb

    def pspec(seg):
        return pl.BlockSpec((hb, tc, LANE), lambda b, h, c: (seg * n_hblk + h, rb0 + b * n_chunks + c, 0))

    in_specs = [pspec(s) for s in range(7)]
    in_specs += [
        pl.BlockSpec((CONV_W, hb * LANE), lambda b, h, c: (0, h)),
        pl.BlockSpec((lb_logits.shape[0], hb * LANE), lambda b, h, c: (0, h)),
        pl.BlockSpec((1, LANE), lambda b, h, c: (0, 0)),
    ]
    args = [proj] * 7 + [conv_w, lb_logits, hg_norm]
    if has_cache:
        in_specs += [
            pl.BlockSpec((1, CONV_W - 1, hb * LANE), lambda b, h, c: (b, 0, h)),
            pl.BlockSpec((1, hb, HEAD_DIM, HEAD_DIM), lambda b, h, c: (b, h, 0, 0)),
        ]
        args += [cache, state]
    d_half = n_groups * LANE
    yspec = pl.BlockSpec((tc, hb * LANE), lambda b, h, c: (b * n_chunks + c, h))
    out_specs = [
        yspec, yspec,
        pl.BlockSpec((1, CONV_W - 1, hb * LANE), lambda b, h, c: (b, 0, h)),
        pl.BlockSpec((1, hb, HEAD_DIM, HEAD_DIM), lambda b, h, c: (b, h, 0, 0)),
    ]
    out_shape = [
        jax.ShapeDtypeStruct((nseq * t, d_half), BF16),
        jax.ShapeDtypeStruct((nseq * t, d_half), BF16),
        jax.ShapeDtypeStruct((nseq, CONV_W - 1, d_half), F32),
        jax.ShapeDtypeStruct((nseq, n_groups, HEAD_DIM, HEAD_DIM), F32),
    ]
    kern = functools.partial(_mixer_kernel, tc=tc, hb=hb, n_chunks=n_chunks, has_cache=has_cache,
                             layer=layer)
    return pl.pallas_call(
        kern,
        grid=(nseq, n_hblk, n_chunks),
        in_specs=in_specs,
        out_specs=out_specs,
        out_shape=out_shape,
        scratch_shapes=[pltpu.VMEM((hb, HEAD_DIM, HEAD_DIM), F32),
                        pltpu.VMEM((hb, SUBLANES + tc, LANE), F32)],
        compiler_params=_params(("parallel", "parallel", "arbitrary")),
        name="mixer_cached" if has_cache else "mixer_prompt",
    )(*args)


def _out_proj_kernel(yap_ref, ybp_ref, yas_ref, ybs_ref, w_ref, xp_ref, xs_ref, wn_ref,
                     o_ref, xw_ref, ssq_ref, *, n_p):
    i = pl.program_id(1)

    def emit(ya_ref, yb_ref, x_ref):
        y = jnp.concatenate([ya_ref[...], yb_ref[...]], axis=1)
        x1 = x_ref[...] + jnp.dot(y, w_ref[...], preferred_element_type=F32)
        o_ref[...] = x1
        xw_ref[...] = (x1 * wn_ref[...]).astype(xw_ref.dtype)
        sq = x1 * x1
        acc = sq[:, 0:LANE]
        for g in range(1, sq.shape[1] // LANE):
            acc = acc + sq[:, g * LANE:(g + 1) * LANE]
        ssq_ref[...] = acc

    @pl.when(i < n_p)
    def _():
        emit(yap_ref, ybp_ref, xp_ref)

    @pl.when(i >= n_p)
    def _():
        emit(yas_ref, ybs_ref, xs_ref)


def _out_proj(ya_p, yb_p, ya_s, yb_s, w, xp, xs, w_norm, tm, tn):
    mp, dh = ya_p.shape
    ms = ya_s.shape[0]
    kdim, n = w.shape
    n_p, n_s = mp // tm, ms // tm
    m = mp + ms

    def prow(j, i):
        return jnp.minimum(i, n_p - 1)

    def srow(j, i):
        return jnp.maximum(i - n_p, 0)

    return pl.pallas_call(
        functools.partial(_out_proj_kernel, n_p=n_p),
        grid=(n // tn, n_p + n_s),
        in_specs=[
            pl.BlockSpec((tm, dh), lambda j, i: (prow(j, i), 0)),
            pl.BlockSpec((tm, dh), lambda j, i: (prow(j, i), 0)),
            pl.BlockSpec((tm, dh), lambda j, i: (srow(j, i), 0)),
            pl.BlockSpec((tm, dh), lambda j, i: (srow(j, i), 0)),
            pl.BlockSpec((kdim, tn), lambda j, i: (0, j)),
            pl.BlockSpec((tm, tn), lambda j, i: (prow(j, i), j)),
            pl.BlockSpec((tm, tn), lambda j, i: (srow(j, i), j)),
            pl.BlockSpec((1, tn), lambda j, i: (0, j)),
        ],
        out_specs=[
            pl.BlockSpec((tm, tn), lambda j, i: (i, j)),
            pl.BlockSpec((tm, tn), lambda j, i: (i, j)),
            pl.BlockSpec((tm, LANE), lambda j, i: (i, j)),
        ],
        out_shape=[
            jax.ShapeDtypeStruct((m, n), F32),
            jax.ShapeDtypeStruct((m, n), BF16),
            jax.ShapeDtypeStruct((m, (n // tn) * LANE), F32),
        ],
        compiler_params=_params(("parallel", "parallel")),
        name="out_proj",
    )(ya_p, yb_p, ya_s, yb_s, w, xp, xs, w_norm)


FFN_UP_SUB_ROWS = 256


def _ffn_up_kernel(a_ref, ssq_ref, wg_ref, wu_ref, o_ref, *, n_valid, sub_rows):
    j = pl.program_id(0)

    @pl.when(j < n_valid)
    def _():
        wg = wg_ref[...].astype(BF16)
        wu = wu_ref[...].astype(BF16)
        for r0 in range(0, a_ref.shape[0], sub_rows):
            rows = pl.ds(r0, sub_rows)
            a = a_ref[rows, :]
            r = lax.rsqrt(jnp.sum(ssq_ref[rows, :], axis=-1, keepdims=True) / a.shape[1] + NORM_EPS)
            g = r * jnp.dot(a, wg, preferred_element_type=F32)
            u = r * jnp.dot(a, wu, preferred_element_type=F32)
            o_ref[rows, :] = (_silu(g) * u).astype(o_ref.dtype)

    @pl.when(j >= n_valid)
    def _():
        o_ref[...] = jnp.zeros_like(o_ref)


def _ffn_up(xw, ssq, wg, wu, tm, tn, f_pad):
    m, k = xw.shape
    n_valid = wg.shape[1] // tn
    wspec = pl.BlockSpec((k, tn), lambda j, i: (0, jnp.minimum(j, n_valid - 1)))
    return pl.pallas_call(
        functools.partial(_ffn_up_kernel, n_valid=n_valid,
                          sub_rows=_largest_tile(tm, FFN_UP_SUB_ROWS, BF16_ROWS)),
        grid=(f_pad // tn, m // tm),
        in_specs=[
            pl.BlockSpec((tm, k), lambda j, i: (i, 0)),
            pl.BlockSpec((tm, ssq.shape[1]), lambda j, i: (i, 0)),
            wspec, wspec,
        ],
        out_specs=pl.BlockSpec((tm, tn), lambda j, i: (i, j)),
        out_shape=jax.ShapeDtypeStruct((m, f_pad), BF16),
        compiler_params=_params(("parallel", "parallel")),
        name="ffn_up",
    )(xw, ssq, wg, wu)


def _ffn_down_kernel(a_ref, w_ref, x_ref, o_ref, *, k_valid):
    rows = lax.broadcasted_iota(jnp.int32, w_ref.shape, 0)
    w = jnp.where(rows < k_valid, w_ref[...], jnp.zeros_like(w_ref))
    o_ref[...] = x_ref[...] + jnp.dot(a_ref[...], w, preferred_element_type=F32)


def _ffn_down(act, w, x1, tm, tn):
    m, kf = act.shape
    n = w.shape[1]
    return pl.pallas_call(
        functools.partial(_ffn_down_kernel, k_valid=w.shape[0]),
        grid=(n // tn, m // tm),
        in_specs=[
            pl.BlockSpec((tm, kf), lambda j, i: (i, 0)),
            pl.BlockSpec((kf, tn), lambda j, i: (0, j)),
            pl.BlockSpec((tm, tn), lambda j, i: (i, j)),
        ],
        out_specs=pl.BlockSpec((tm, tn), lambda j, i: (i, j)),
        out_shape=jax.ShapeDtypeStruct((m, n), F32),
        compiler_params=_params(("parallel", "parallel")),
        name="ffn_down",
    )(act, w, x1)


def _mixer_tiles(t, n_groups):
    tc = _largest_tile(t, MIXER_BLOCKS_PER_STEP * SCAN_BLOCK, SCAN_BLOCK)
    hb = 1
    while hb < n_groups and 2 * hb * tc <= MIXER_BLOCKS_PER_STEP * SCAN_BLOCK and n_groups % (2 * hb) == 0:
        hb *= 2
    return dict(tc=tc, hb=hb)


def _tiles(mp, ms, tp, ts, n_groups, d, n_in, d_ff, f_pad):
    m = mp + ms
    tr = _largest_tile(_gcd(mp, ms), 512, BF16_ROWS)
    return dict(
        tr=tr,
        tm=_largest_tile(m, 1536, BF16_ROWS),
        tn_in=_largest_tile(n_in, 512, LANE),
        tn_out=_largest_tile(d, 1024, LANE),
        tn_up=_largest_tile(_gcd(d_ff, f_pad), 256, LANE),
        tm_down=_largest_tile(m, 528, BF16_ROWS),
        tn_down=_largest_tile(d, 512, LANE),
        mix_p=_mixer_tiles(tp, n_groups),
        mix_s=_mixer_tiles(ts, n_groups),
    )


def _gcd(a, b):
    while b:
        a, b = b, a % b
    return a


def kernel(x_prompt, x_sample, cache_conv, state_hgrn, norm_mix, w_in, conv_w, lb_logits, hg_norm,
           w_out, norm_ffn, w_gate, w_up, w_down, norm_final):
    bp, tp, d = x_prompt.shape
    bs, ts, _ = x_sample.shape
    depth = w_in.shape[0]
    if depth != 1:
        raise NotImplementedError("one layer")
    layer = 0
    n_in = w_in.shape[2]
    d_ff = w_gate.shape[2]
    d_half = conv_w.shape[2]
    n_groups = d_half // LANE
    if n_in != 7 * d_half or state_hgrn.shape[2] != n_groups or w_out.shape[1] != 2 * d_half:
        raise ValueError("unexpected layer geometry")
    mp, ms = bp * tp, bs * ts
    m = mp + ms
    f_pad = _round_up(d_ff, 1024)
    t = _tiles(mp, ms, tp, ts, n_groups, d, n_in, d_ff, f_pad)

    xp = x_prompt.reshape(mp, d)
    xs = x_sample.reshape(ms, d)
    w_out_b = w_out[layer].astype(BF16)
    w_down_b = w_down[layer].astype(BF16)

    h = _norm_two_src(xp, xs, norm_mix[layer].reshape(1, d), t["tr"])
    proj = _in_proj(h, w_in[layer], t["tm"], t["tn_in"])

    mix_args = dict(n_groups=n_groups, layer=layer)
    hgn = hg_norm[layer].reshape(1, HEAD_DIM)
    ya_p, yb_p, conv_p, hg_p = _mixer(proj, conv_w[layer], lb_logits, hgn, row0=0, nseq=bp, t=tp,
                                      **t["mix_p"], **mix_args)
    ya_s, yb_s, conv_s, hg_s = _mixer(proj, conv_w[layer], lb_logits, hgn, row0=mp, nseq=bs, t=ts,
                                      cache=cache_conv[layer], state=state_hgrn[layer],
                                      **t["mix_s"], **mix_args)

    x1, x1w, ssq = _out_proj(ya_p, yb_p, ya_s, yb_s, w_out_b, xp, xs, norm_ffn[layer].reshape(1, d),
                             t["tr"], t["tn_out"])
    act = _ffn_up(x1w, ssq, w_gate[layer], w_up[layer], t["tm"], t["tn_up"], f_pad)
    x2 = _ffn_down(act, w_down_b, x1, t["tm_down"], t["tn_down"])
    y_p, y_s = _norm_two_sink(x2, norm_final.reshape(1, d), mp, t["tr"])

    return (y_p.reshape(bp, tp, d), y_s.reshape(bs, ts, d),
            conv_p[None], hg_p[None].astype(x_prompt.dtype),
            conv_s[None].astype(cache_conv.dtype), hg_s[None].astype(state_hgrn.dtype))
```

```python
import functools

import jax
import jax.numpy as jnp
from jax import lax
from jax.experimental import pallas as pl
from jax.experimental.pallas import tpu as pltpu

LANE = 128
SUBLANES = 8
BF16_ROWS = 16
SCAN_BLOCK = 16
HEAD_DIM = 128
CONV_W = 3
NORM_EPS = 1e-6
V7X_VMEM_BYTES = 64 * 1024 * 1024
VMEM_LIMIT = V7X_VMEM_BYTES - 6 * 1024 * 1024

F32 = jnp.float32
BF16 = jnp.bfloat16


def _largest_tile(n, cap, quantum):
    best = None
    for t in range(quantum, min(n, cap) + 1, quantum):
        if n % t == 0:
            best = t
    if best is None:
        raise ValueError(f"no tile for n={n} cap={cap} quantum={quantum}")
    return best


def _round_up(n, m):
    return (n + m - 1) // m * m


def _gcd(a, b):
    while b:
        a, b = b, a % b
    return a


def _params(sem):
    return pltpu.CompilerParams(dimension_semantics=sem, vmem_limit_bytes=VMEM_LIMIT)


def _rms_rows(x, w):
    ms = jnp.mean(x * x, axis=-1, keepdims=True)
    return x * lax.rsqrt(ms + NORM_EPS) * w


def _silu(x):
    return x * jax.nn.sigmoid(x)


def _norm_two_src_kernel(xp_ref, xs_ref, w_ref, o_ref, *, n_p):
    i = pl.program_id(0)

    @pl.when(i < n_p)
    def _():
        o_ref[...] = _rms_rows(xp_ref[...], w_ref[...]).astype(o_ref.dtype)

    @pl.when(i >= n_p)
    def _():
        o_ref[...] = _rms_rows(xs_ref[...], w_ref[...]).astype(o_ref.dtype)


def _norm_two_src(xp, xs, w, tr):
    mp, d = xp.shape
    ms = xs.shape[0]
    n_p, n_s = mp // tr, ms // tr
    return pl.pallas_call(
        functools.partial(_norm_two_src_kernel, n_p=n_p),
        grid=(n_p + n_s,),
        in_specs=[
            pl.BlockSpec((tr, d), lambda i: (jnp.minimum(i, n_p - 1), 0)),
            pl.BlockSpec((tr, d), lambda i: (jnp.maximum(i - n_p, 0), 0)),
            pl.BlockSpec((1, d), lambda i: (0, 0)),
        ],
        out_specs=pl.BlockSpec((tr, d), lambda i: (i, 0)),
        out_shape=jax.ShapeDtypeStruct((mp + ms, d), BF16),
        compiler_params=_params(("parallel",)),
        name="norm1",
    )(xp, xs, w)


def _norm_two_sink_kernel(x_ref, w_ref, op_ref, os_ref, *, n_p):
    i = pl.program_id(0)

    @pl.when(i < n_p)
    def _():
        op_ref[...] = _rms_rows(x_ref[...], w_ref[...])

    @pl.when(i >= n_p)
    def _():
        os_ref[...] = _rms_rows(x_ref[...], w_ref[...])


def _norm_two_sink(x, w, mp, tr):
    m, d = x.shape
    ms = m - mp
    n_p, n_s = mp // tr, ms // tr
    return pl.pallas_call(
        functools.partial(_norm_two_sink_kernel, n_p=n_p),
        grid=(n_p + n_s,),
        in_specs=[pl.BlockSpec((tr, d), lambda i: (i, 0)), pl.BlockSpec((1, d), lambda i: (0, 0))],
        out_specs=[
            pl.BlockSpec((tr, d), lambda i: (jnp.minimum(i, n_p - 1), 0)),
            pl.BlockSpec((tr, d), lambda i: (jnp.maximum(i - n_p, 0), 0)),
        ],
        out_shape=[jax.ShapeDtypeStruct((mp, d), F32), jax.ShapeDtypeStruct((ms, d), F32)],
        compiler_params=_params(("arbitrary",)),
        name="final_norm",
    )(x, w)


def _cast_rows(r_in, r_out, n_steps):
    g = _gcd(r_in, r_out)
    for rows in range(BF16_ROWS, g + 1, BF16_ROWS):
        if g % rows == 0 and r_out // rows <= n_steps:
            return rows
    raise ValueError(f"cannot cast {r_in} -> {r_out} rows in {n_steps} grid steps")


def _in_proj_kernel(*refs, n_cast, cast_blocks):
    a_ref, b_ref = refs[0:2]
    cast_in = refs[2:2 + n_cast]
    o_ref = refs[2 + n_cast]
    cast_out = refs[3 + n_cast:3 + 2 * n_cast]
    step = pl.program_id(0) * pl.num_programs(1) + pl.program_id(1)
    for src, dst, (n_src, n_dst) in zip(cast_in, cast_out, cast_blocks):
        val = src[...].astype(dst.dtype)
        if n_dst > n_src:
            val = jnp.where(step < n_src, val, jnp.zeros_like(val))
        dst[...] = val

    acc = jnp.dot(a_ref[...], b_ref[...].astype(BF16), preferred_element_type=F32)
    for g in range(o_ref.shape[0]):
        o_ref[g] = acc[:, g * LANE:(g + 1) * LANE]


def _in_proj(h, w, tm, tn, casts):
    m, k = h.shape
    n = w.shape[1]
    ni, nj = m // tm, n // tn
    in_specs = [pl.BlockSpec((tm, k), lambda i, j: (i, 0)), pl.BlockSpec((k, tn), lambda i, j: (0, j))]
    out_specs = [pl.BlockSpec((tn // LANE, tm, LANE), lambda i, j: (j, i, 0))]
    out_shape = [jax.ShapeDtypeStruct((n // LANE, m, LANE), F32)]
    cast_blocks = []
    for arr, r_out in casts:
        r_in, c = arr.shape
        rows = _cast_rows(r_in, r_out, ni * nj)
        n_src, n_dst = r_in // rows, r_out // rows
        cast_blocks.append((n_src, n_dst))
        in_specs.append(pl.BlockSpec((rows, c), lambda i, j, n_src=n_src: (jnp.minimum(i * nj + j, n_src - 1), 0)))
        out_specs.append(pl.BlockSpec((rows, c), lambda i, j, n_dst=n_dst: (jnp.minimum(i * nj + j, n_dst - 1), 0)))
        out_shape.append(jax.ShapeDtypeStruct((r_out, c), BF16))
    outs = pl.pallas_call(
        functools.partial(_in_proj_kernel, n_cast=len(casts), cast_blocks=tuple(cast_blocks)),
        grid=(ni, nj),
        in_specs=in_specs,
        out_specs=out_specs,
        out_shape=out_shape,
        compiler_params=_params(("arbitrary", "arbitrary")),
        name="in_proj",
    )(h, w, *[arr for arr, _ in casts])
    return outs[0], outs[1:]


STATE_LAG = 4
MIXER_BLOCKS_PER_STEP = 256


def _mixer_kernel(*refs, tc, hb, n_chunks, has_cache, layer):
    if has_cache:
        (bg_ref, cg_ref, u_ref, q_ref, fz_ref, iv_ref, og_ref, convw_ref, lbl_ref, hgn_ref,
         cbuf_ref, s0_ref, ya_ref, yb_ref, nconv_ref, nst_ref, st_ref, cu_ref) = refs
    else:
        (bg_ref, cg_ref, u_ref, q_ref, fz_ref, iv_ref, og_ref, convw_ref, lbl_ref, hgn_ref,
         ya_ref, yb_ref, nconv_ref, nst_ref, st_ref, cu_ref) = refs
    c = pl.program_id(2)
    nsb = tc // SCAN_BLOCK

    @pl.when(c == 0)
    def _init():
        for hh in range(hb):
            if has_cache:
                st_ref[hh] = s0_ref[0, hh].T
                cu_ref[hh, SUBLANES - (CONV_W - 1):SUBLANES, :] = cbuf_ref[0, :, hh * LANE:(hh + 1) * LANE]
            else:
                st_ref[hh] = jnp.zeros((HEAD_DIM, HEAD_DIM), F32)
                cu_ref[hh, 0:SUBLANES, :] = jnp.zeros((SUBLANES, LANE), F32)

    hgn = hgn_ref[...]
    sub = lax.broadcasted_iota(jnp.int32, (SUBLANES, LANE), 0)
    tri = (lax.broadcasted_iota(jnp.int32, (SCAN_BLOCK, SCAN_BLOCK), 0)
           >= lax.broadcasted_iota(jnp.int32, (SCAN_BLOCK, SCAN_BLOCK), 1))
    mid = SCAN_BLOCK // 2
    nt = (((1,), (1,)), ((), ()))
    tn = (((0,), (0,)), ((), ()))

    def head_consts(hh):
        lanes = slice(hh * LANE, (hh + 1) * LANE)
        logits = lbl_ref[:, lanes]
        ex = jnp.exp(logits - jnp.max(logits, axis=0, keepdims=True))
        lb = jnp.sum(ex[0:layer + 1, :], axis=0, keepdims=True) / jnp.sum(ex, axis=0, keepdims=True)
        return lb, 1.0 - lb, convw_ref[0:1, lanes], convw_ref[1:2, lanes], convw_ref[2:3, lanes]

    consts = [head_consts(hh) for hh in range(hb)]

    def prefix8(x):
        sh = 1
        while sh < SUBLANES:
            x = x + jnp.where(sub >= sh, pltpu.roll(x, sh, 0), 0.0)
            sh *= 2
        return x

    def conv_block(hh, lo):
        _, _, w0, w1, w2 = consts[hh]
        rows = pl.ds(lo, SCAN_BLOCK)
        cu = cg_ref[hh, rows, :] * u_ref[hh, rows, :]
        cu_ref[hh, pl.ds(SUBLANES + lo, SCAN_BLOCK), :] = cu
        s1 = cu_ref[hh, pl.ds(SUBLANES - 1 + lo, SCAN_BLOCK), :]
        s2 = cu_ref[hh, pl.ds(SUBLANES - 2 + lo, SCAN_BLOCK), :]
        conv = w0 * s2 + w1 * s1 + w2 * cu
        ya_ref[rows, hh * LANE:(hh + 1) * LANE] = (bg_ref[hh, rows, :] * conv).astype(ya_ref.dtype)

    def state_free(hh, lo):
        lb, one_m_lb = consts[hh][0:2]
        rows = pl.ds(lo, SCAN_BLOCK)
        f = lb + one_m_lb * jax.nn.sigmoid(fz_ref[hh, rows, :])
        g2 = jnp.log2(f)
        kj = 1.0 - f
        qj = _silu(q_ref[hh, rows, :])
        vj = iv_ref[hh, rows, :].astype(BF16)
        parts = []
        total = None
        for p in range(SCAN_BLOCK // SUBLANES):
            part = prefix8(g2[p * SUBLANES:(p + 1) * SUBLANES, :])
            if total is not None:
                part = part + total
            total = part[SUBLANES - 1:SUBLANES, :]
            parts.append(part)
        lcj = jnp.concatenate(parts, axis=0)
        ref = lcj[mid - 1:mid, :]
        llast = total
        qd = (qj * jnp.exp2(lcj - ref)).astype(BF16)
        kd = (kj * jnp.exp2(ref - lcj)).astype(BF16)
        att = lax.dot_general(qd, kd, nt, preferred_element_type=F32)
        att = jnp.where(tri, att, 0.0).astype(BF16)
        o_intra = jnp.dot(att, vj, preferred_element_type=F32)
        kdec = (kj * jnp.exp2(llast - lcj)).astype(BF16)
        upd = lax.dot_general(vj, kdec, tn, preferred_element_type=F32)
        return o_intra, upd, jnp.exp2(llast), (qj * jnp.exp2(lcj)).astype(BF16)

    blocks = [(hh, j * SCAN_BLOCK) for hh in range(hb) for j in range(nsb)]
    st = [st_ref[hh] for hh in range(hb)]
    pending = {}
    for n in range(len(blocks) + STATE_LAG):
        if n < len(blocks):
            conv_block(*blocks[n])
            pending[n] = state_free(*blocks[n])
        if n >= STATE_LAG:
            hh, lo = blocks[n - STATE_LAG]
            o_intra, upd, decay, qe = pending.pop(n - STATE_LAG)
            o = o_intra + lax.dot_general(qe, st[hh].astype(BF16), nt, preferred_element_type=F32)
            st[hh] = st[hh] * decay + upd
            rows = pl.ds(lo, SCAN_BLOCK)
            ms = jnp.mean(o * o, axis=-1, keepdims=True)
            yb = o * lax.rsqrt(ms + NORM_EPS) * (hgn * _silu(og_ref[hh, rows, :]))
            yb_ref[rows, hh * LANE:(hh + 1) * LANE] = yb.astype(yb_ref.dtype)
    for hh in range(hb):
        st_ref[hh] = st[hh]
        tail = cu_ref[hh, SUBLANES + tc - (CONV_W - 1):SUBLANES + tc, :]
        cu_ref[hh, SUBLANES - (CONV_W - 1):SUBLANES, :] = tail
        nconv_ref[0, :, hh * LANE:(hh + 1) * LANE] = tail

    @pl.when(c == n_chunks - 1)
    def _fin():
        for hh in range(hb):
            nst_ref[0, hh] = st[hh].T


def _mixer(proj, conv_w, lb_logits, hg_norm, *, row0, nseq, t, tc, hb, n_groups, layer, cache=None, state=None):
    has_cache = cache is not None
    if has_cache != (state is not None):
        raise ValueError("conv cache and recurrent state come together")
    n_chunks = t // tc
    rb0 = row0 // tc
    n_hblk = n_groups // hb

    def pspec(seg):
        return pl.BlockSpec((hb, tc, LANE), lambda b, h, c: (seg * n_hblk + h, rb0 + b * n_chunks + c, 0))

    in_specs = [pspec(s) for s in range(7)]
    in_specs += [
        pl.BlockSpec((CONV_W, hb * LANE), lambda b, h, c: (0, h)),
        pl.BlockSpec((lb_logits.shape[0], hb * LANE), lambda b, h, c: (0, h)),
        pl.BlockSpec((1, LANE), lambda b, h, c: (0, 0)),
    ]
    args = [proj] * 7 + [conv_w, lb_logits, hg_norm]
    if has_cache:
        in_specs += [
            pl.BlockSpec((1, CONV_W - 1, hb * LANE), lambda b, h, c: (b, 0, h)),
            pl.BlockSpec((1, hb, HEAD_DIM, HEAD_DIM), lambda b, h, c: (b, h, 0, 0)),
        ]
        args += [cache, state]
    d_half = n_groups * LANE
    yspec = pl.BlockSpec((tc, hb * LANE), lambda b, h, c: (b * n_chunks + c, h))
    out_specs = [
        yspec, yspec,
        pl.BlockSpec((1, CONV_W - 1, hb * LANE), lambda b, h, c: (b, 0, h)),
        pl.BlockSpec((1, hb, HEAD_DIM, HEAD_DIM), lambda b, h, c: (b, h, 0, 0)),
    ]
    out_shape = [
        jax.ShapeDtypeStruct((nseq * t, d_half), BF16),
        jax.ShapeDtypeStruct((nseq * t, d_half), BF16),
        jax.ShapeDtypeStruct((nseq, CONV_W - 1, d_half), F32),
        jax.ShapeDtypeStruct((nseq, n_groups, HEAD_DIM, HEAD_DIM), F32),
    ]
    kern = functools.partial(_mixer_kernel, tc=tc, hb=hb, n_chunks=n_chunks, has_cache=has_cache,
                             layer=layer)
    return pl.pallas_call(
        kern,
        grid=(nseq, n_hblk, n_chunks),
        in_specs=in_specs,
        out_specs=out_specs,
        out_shape=out_shape,
        scratch_shapes=[pltpu.VMEM((hb, HEAD_DIM, HEAD_DIM), F32),
                        pltpu.VMEM((hb, SUBLANES + tc, LANE), F32)],
        compiler_params=_params(("parallel", "parallel", "arbitrary")),
        name="mixer_cached" if has_cache else "mixer_prompt",
    )(*args)


def _out_proj_kernel(yap_ref, ybp_ref, yas_ref, ybs_ref, w_ref, xp_ref, xs_ref, wn_ref,
                     o_ref, xw_ref, ssq_ref, *, n_p):
    i = pl.program_id(1)

    def emit(ya_ref, yb_ref, x_ref):
        y = jnp.concatenate([ya_ref[...], yb_ref[...]], axis=1)
        x1 = x_ref[...] + jnp.dot(y, w_ref[...], preferred_element_type=F32)
        o_ref[...] = x1
        xw_ref[...] = (x1 * wn_ref[...]).astype(xw_ref.dtype)
        sq = x1 * x1
        acc = sq[:, 0:LANE]
        for g in range(1, sq.shape[1] // LANE):
            acc = acc + sq[:, g * LANE:(g + 1) * LANE]
        ssq_ref[...] = acc

    @pl.when(i < n_p)
    def _():
        emit(yap_ref, ybp_ref, xp_ref)

    @pl.when(i >= n_p)
    def _():
        emit(yas_ref, ybs_ref, xs_ref)


def _out_proj(ya_p, yb_p, ya_s, yb_s, w, xp, xs, w_norm, tm, tn):
    mp, dh = ya_p.shape
    ms = ya_s.shape[0]
    kdim, n = w.shape
    n_p, n_s = mp // tm, ms // tm
    m = mp + ms

    def prow(j, i):
        return jnp.minimum(i, n_p - 1)

    def srow(j, i):
        return jnp.maximum(i - n_p, 0)

    return pl.pallas_call(
        functools.partial(_out_proj_kernel, n_p=n_p),
        grid=(n // tn, n_p + n_s),
        in_specs=[
            pl.BlockSpec((tm, dh), lambda j, i: (prow(j, i), 0)),
            pl.BlockSpec((tm, dh), lambda j, i: (prow(j, i), 0)),
            pl.BlockSpec((tm, dh), lambda j, i: (srow(j, i), 0)),
            pl.BlockSpec((tm, dh), lambda j, i: (srow(j, i), 0)),
            pl.BlockSpec((kdim, tn), lambda j, i: (0, j)),
            pl.BlockSpec((tm, tn), lambda j, i: (prow(j, i), j)),
            pl.BlockSpec((tm, tn), lambda j, i: (srow(j, i), j)),
            pl.BlockSpec((1, tn), lambda j, i: (0, j)),
        ],
        out_specs=[
            pl.BlockSpec((tm, tn), lambda j, i: (i, j)),
            pl.BlockSpec((tm, tn), lambda j, i: (i, j)),
            pl.BlockSpec((tm, LANE), lambda j, i: (i, j)),
        ],
        out_shape=[
            jax.ShapeDtypeStruct((m, n), F32),
            jax.ShapeDtypeStruct((m, n), BF16),
            jax.ShapeDtypeStruct((m, (n // tn) * LANE), F32),
        ],
        compiler_params=_params(("parallel", "parallel")),
        name="out_proj",
    )(ya_p, yb_p, ya_s, yb_s, w, xp, xs, w_norm)


FFN_UP_SUB_ROWS = 256


def _ffn_up_kernel(a_ref, ssq_ref, wg_ref, wu_ref, o_ref, *, n_valid, sub_rows):
    j = pl.program_id(0)

    @pl.when(j < n_valid)
    def _():
        wg = wg_ref[...].astype(BF16)
        wu = wu_ref[...].astype(BF16)
        for r0 in range(0, a_ref.shape[0], sub_rows):
            rows = pl.ds(r0, sub_rows)
            a = a_ref[rows, :]
            r = lax.rsqrt(jnp.sum(ssq_ref[rows, :], axis=-1, keepdims=True) / a.shape[1] + NORM_EPS)
            g = r * jnp.dot(a, wg, preferred_element_type=F32)
            u = r * jnp.dot(a, wu, preferred_element_type=F32)
            o_ref[rows, :] = (_silu(g) * u).astype(o_ref.dtype)

    @pl.when(j >= n_valid)
    def _():
        o_ref[...] = jnp.zeros_like(o_ref)


def _ffn_up(xw, ssq, wg, wu, tm, tn, f_pad):
    m, k = xw.shape
    n_valid = wg.shape[1] // tn
    wspec = pl.BlockSpec((k, tn), lambda j, i: (0, jnp.minimum(j, n_valid - 1)))
    return pl.pallas_call(
        functools.partial(_ffn_up_kernel, n_valid=n_valid,
                          sub_rows=_largest_tile(tm, FFN_UP_SUB_ROWS, BF16_ROWS)),
        grid=(f_pad // tn, m // tm),
        in_specs=[
            pl.BlockSpec((tm, k), lambda j, i: (i, 0)),
            pl.BlockSpec((tm, ssq.shape[1]), lambda j, i: (i, 0)),
            wspec, wspec,
        ],
        out_specs=pl.BlockSpec((tm, tn), lambda j, i: (i, j)),
        out_shape=jax.ShapeDtypeStruct((m, f_pad), BF16),
        compiler_params=_params(("parallel", "parallel")),
        name="ffn_up",
    )(xw, ssq, wg, wu)


def _ffn_down_kernel(a_ref, w_ref, x_ref, o_ref):
    o_ref[...] = x_ref[...] + jnp.dot(a_ref[...], w_ref[...], preferred_element_type=F32)


def _ffn_down(act, w, x1, tm, tn):
    m, kf = act.shape
    n = w.shape[1]
    return pl.pallas_call(
        _ffn_down_kernel,
        grid=(n // tn, m // tm),
        in_specs=[
            pl.BlockSpec((tm, kf), lambda j, i: (i, 0)),
            pl.BlockSpec((kf, tn), lambda j, i: (0, j)),
            pl.BlockSpec((tm, tn), lambda j, i: (i, j)),
        ],
        out_specs=pl.BlockSpec((tm, tn), lambda j, i: (i, j)),
        out_shape=jax.ShapeDtypeStruct((m, n), F32),
        compiler_params=_params(("parallel", "parallel")),
        name="ffn_down",
    )(act, w, x1)


def _mixer_tiles(t, n_groups):
    tc = _largest_tile(t, MIXER_BLOCKS_PER_STEP * SCAN_BLOCK, SCAN_BLOCK)
    hb = 1
    while hb < n_groups and 2 * hb * tc <= MIXER_BLOCKS_PER_STEP * SCAN_BLOCK and n_groups % (2 * hb) == 0:
        hb *= 2
    return dict(tc=tc, hb=hb)


def _tiles(mp, ms, tp, ts, n_groups, d, n_in, d_ff, f_pad):
    m = mp + ms
    tr = _largest_tile(_gcd(mp, ms), 512, BF16_ROWS)
    return dict(
        tr=tr,
        tm=_largest_tile(m, 1536, BF16_ROWS),
        tn_in=_largest_tile(n_in, 512, LANE),
        tn_out=_largest_tile(d, 1024, LANE),
        tn_up=_largest_tile(_gcd(d_ff, f_pad), 256, LANE),
        tm_down=_largest_tile(m, 528, BF16_ROWS),
        tn_down=_largest_tile(d, 512, LANE),
        mix_p=_mixer_tiles(tp, n_groups),
        mix_s=_mixer_tiles(ts, n_groups),
    )


def kernel(x_prompt, x_sample, cache_conv, state_hgrn, norm_mix, w_in, conv_w, lb_logits, hg_norm,
           w_out, norm_ffn, w_gate, w_up, w_down, norm_final):
    bp, tp, d = x_prompt.shape
    bs, ts, _ = x_sample.shape
    depth = w_in.shape[0]
    if depth != 1:
        raise NotImplementedError("one layer")
    layer = 0
    n_in = w_in.shape[2]
    d_ff = w_gate.shape[2]
    d_half = conv_w.shape[2]
    n_groups = d_half // LANE
    if n_in != 7 * d_half or state_hgrn.shape[2] != n_groups or w_out.shape[1] != 2 * d_half:
        raise ValueError("unexpected layer geometry")
    mp, ms = bp * tp, bs * ts
    m = mp + ms
    f_pad = _round_up(d_ff, 1024)
    t = _tiles(mp, ms, tp, ts, n_groups, d, n_in, d_ff, f_pad)

    xp = x_prompt.reshape(mp, d)
    xs = x_sample.reshape(ms, d)
    h = _norm_two_src(xp, xs, norm_mix[layer].reshape(1, d), t["tr"])
    proj, (w_out_b, w_down_b) = _in_proj(h, w_in[layer], t["tm"], t["tn_in"],
                                         casts=[(w_out[layer], w_out.shape[1]), (w_down[layer], f_pad)])

    mix_args = dict(n_groups=n_groups, layer=layer)
    hgn = hg_norm[layer].reshape(1, HEAD_DIM)
    ya_p, yb_p, conv_p, hg_p = _mixer(proj, conv_w[layer], lb_logits, hgn, row0=0, nseq=bp, t=tp,
                                      **t["mix_p"], **mix_args)
    ya_s, yb_s, conv_s, hg_s = _mixer(proj, conv_w[layer], lb_logits, hgn, row0=mp, nseq=bs, t=ts,
                                      cache=cache_conv[layer], state=state_hgrn[layer],
                                      **t["mix_s"], **mix_args)

    x1, x1w, ssq = _out_proj(ya_p, yb_p, ya_s, yb_s, w_out_b, xp, xs, norm_ffn[layer].reshape(1, d),
                             t["tr"], t["tn_out"])
    act = _ffn_up(x1w, ssq, w_gate[layer], w_up[layer], t["tm"], t["tn_up"], f_pad)
    x2 = _ffn_down(act, w_down_b, x1, t["tm_down"], t["tn_down"])
    y_p, y_s = _norm_two_sink(x2, norm_final.reshape(1, d), mp, t["tr"])

    return (y_p.reshape(bp, tp, d), y_s.reshape(bs, ts, d),
            conv_p[None], hg_p[None].astype(x_prompt.dtype),
            conv_s[None].astype(cache_conv.dtype), hg_s[None].astype(state_hgrn.dtype))
```

```python
import functools

import jax
import jax.numpy as jnp
from jax import lax
from jax.experimental import pallas as pl
from jax.experimental.pallas import tpu as pltpu

LANE = 128
SUBLANES = 8
BF16_ROWS = 16
SCAN_BLOCK = 16
HEAD_DIM = 128
CONV_W = 3
NORM_EPS = 1e-6
V7X_VMEM_BYTES = 64 * 1024 * 1024
VMEM_LIMIT = V7X_VMEM_BYTES - 6 * 1024 * 1024

F32 = jnp.float32
BF16 = jnp.bfloat16


def _largest_tile(n, cap, quantum):
    best = None
    for t in range(quantum, min(n, cap) + 1, quantum):
        if n % t == 0:
            best = t
    if best is None:
        raise ValueError(f"no tile for n={n} cap={cap} quantum={quantum}")
    return best


def _round_up(n, m):
    return (n + m - 1) // m * m


def _gcd(a, b):
    while b:
        a, b = b, a % b
    return a


def _params(sem):
    return pltpu.CompilerParams(dimension_semantics=sem, vmem_limit_bytes=VMEM_LIMIT)


def _rms_rows(x, w):
    ms = jnp.mean(x * x, axis=-1, keepdims=True)
    return x * lax.rsqrt(ms + NORM_EPS) * w


def _silu(x):
    return x * jax.nn.sigmoid(x)


def _norm_two_src_kernel(xp_ref, xs_ref, w_ref, o_ref, *, n_p):
    i = pl.program_id(0)

    @pl.when(i < n_p)
    def _():
        o_ref[...] = _rms_rows(xp_ref[...], w_ref[...]).astype(o_ref.dtype)

    @pl.when(i >= n_p)
    def _():
        o_ref[...] = _rms_rows(xs_ref[...], w_ref[...]).astype(o_ref.dtype)


def _norm_two_src(xp, xs, w, tr):
    mp, d = xp.shape
    ms = xs.shape[0]
    n_p, n_s = mp // tr, ms // tr
    return pl.pallas_call(
        functools.partial(_norm_two_src_kernel, n_p=n_p),
        grid=(n_p + n_s,),
        in_specs=[
            pl.BlockSpec((tr, d), lambda i: (jnp.minimum(i, n_p - 1), 0)),
            pl.BlockSpec((tr, d), lambda i: (jnp.maximum(i - n_p, 0), 0)),
            pl.BlockSpec((1, d), lambda i: (0, 0)),
        ],
        out_specs=pl.BlockSpec((tr, d), lambda i: (i, 0)),
        out_shape=jax.ShapeDtypeStruct((mp + ms, d), BF16),
        compiler_params=_params(("parallel",)),
        name="norm1",
    )(xp, xs, w)


def _norm_two_sink_kernel(x_ref, w_ref, op_ref, os_ref, *, n_p):
    i = pl.program_id(0)

    @pl.when(i < n_p)
    def _():
        op_ref[...] = _rms_rows(x_ref[...], w_ref[...])

    @pl.when(i >= n_p)
    def _():
        os_ref[...] = _rms_rows(x_ref[...], w_ref[...])


def _norm_two_sink(x, w, mp, tr):
    m, d = x.shape
    ms = m - mp
    n_p, n_s = mp // tr, ms // tr
    return pl.pallas_call(
        functools.partial(_norm_two_sink_kernel, n_p=n_p),
        grid=(n_p + n_s,),
        in_specs=[pl.BlockSpec((tr, d), lambda i: (i, 0)), pl.BlockSpec((1, d), lambda i: (0, 0))],
        out_specs=[
            pl.BlockSpec((tr, d), lambda i: (jnp.minimum(i, n_p - 1), 0)),
            pl.BlockSpec((tr, d), lambda i: (jnp.maximum(i - n_p, 0), 0)),
        ],
        out_shape=[jax.ShapeDtypeStruct((mp, d), F32), jax.ShapeDtypeStruct((ms, d), F32)],
        compiler_params=_params(("arbitrary",)),
        name="final_norm",
    )(x, w)


def _cast_rows(r_in, r_out, n_steps):
    g = _gcd(r_in, r_out)
    for rows in range(BF16_ROWS, g + 1, BF16_ROWS):
        if g % rows == 0 and r_out // rows <= n_steps:
            return rows
    raise ValueError(f"cannot cast {r_in} -> {r_out} rows in {n_steps} grid steps")


def _in_proj_kernel(*refs, n_cast, cast_blocks):
    a_ref, b_ref = refs[0:2]
    cast_in = refs[2:2 + n_cast]
    o_ref = refs[2 + n_cast]
    cast_out = refs[3 + n_cast:3 + 2 * n_cast]
    step = pl.program_id(0) * pl.num_programs(1) + pl.program_id(1)
    for src, dst, (n_src, n_dst) in zip(cast_in, cast_out, cast_blocks):
        val = src[...].astype(dst.dtype)
        if n_dst > n_src:
            val = jnp.where(step < n_src, val, jnp.zeros_like(val))
        dst[...] = val

    acc = jnp.dot(a_ref[...], b_ref[...].astype(BF16), preferred_element_type=F32)
    for g in range(o_ref.shape[0]):
        o_ref[g] = acc[:, g * LANE:(g + 1) * LANE]


def _in_proj(h, w, tm, tn, casts):
    m, k = h.shape
    n = w.shape[1]
    ni, nj = m // tm, n // tn
    in_specs = [pl.BlockSpec((tm, k), lambda i, j: (i, 0)), pl.BlockSpec((k, tn), lambda i, j: (0, j))]
    out_specs = [pl.BlockSpec((tn // LANE, tm, LANE), lambda i, j: (j, i, 0))]
    out_shape = [jax.ShapeDtypeStruct((n // LANE, m, LANE), F32)]
    cast_blocks = []
    for arr, r_out in casts:
        r_in, c = arr.shape
        rows = _cast_rows(r_in, r_out, ni * nj)
        n_src, n_dst = r_in // rows, r_out // rows
        cast_blocks.append((n_src, n_dst))
        in_specs.append(pl.BlockSpec((rows, c), lambda i, j, n_src=n_src: (jnp.minimum(i * nj + j, n_src - 1), 0)))
        out_specs.append(pl.BlockSpec((rows, c), lambda i, j, n_dst=n_dst: (jnp.minimum(i * nj + j, n_dst - 1), 0)))
        out_shape.append(jax.ShapeDtypeStruct((r_out, c), BF16))
    outs = pl.pallas_call(
        functools.partial(_in_proj_kernel, n_cast=len(casts), cast_blocks=tuple(cast_blocks)),
        grid=(ni, nj),
        in_specs=in_specs,
        out_specs=out_specs,
        out_shape=out_shape,
        compiler_params=_params(("arbitrary", "arbitrary")),
        name="in_proj",
    )(h, w, *[arr for arr, _ in casts])
    return outs[0], outs[1:]


STATE_LAG = 2
MIXER_BLOCKS_PER_STEP = 256


def _mixer_kernel(*refs, tc, hb, n_chunks, has_cache, layer):
    if has_cache:
        (bg_ref, cg_ref, u_ref, q_ref, fz_ref, iv_ref, og_ref, convw_ref, lbl_ref, hgn_ref,
         cbuf_ref, s0_ref, ya_ref, yb_ref, nconv_ref, nst_ref, st_ref, cu_ref) = refs
    else:
        (bg_ref, cg_ref, u_ref, q_ref, fz_ref, iv_ref, og_ref, convw_ref, lbl_ref, hgn_ref,
         ya_ref, yb_ref, nconv_ref, nst_ref, st_ref, cu_ref) = refs
    c = pl.program_id(2)

    @pl.when(c == 0)
    def _init():
        for hh in range(hb):
            if has_cache:
                st_ref[hh] = s0_ref[0, hh].T
                cu_ref[hh, SUBLANES - (CONV_W - 1):SUBLANES, :] = cbuf_ref[0, :, hh * LANE:(hh + 1) * LANE]
            else:
                st_ref[hh] = jnp.zeros((HEAD_DIM, HEAD_DIM), F32)
                cu_ref[hh, 0:SUBLANES, :] = jnp.zeros((SUBLANES, LANE), F32)

    hgn = hgn_ref[...]
    sub = lax.broadcasted_iota(jnp.int32, (SUBLANES, LANE), 0)
    tri = (lax.broadcasted_iota(jnp.int32, (SCAN_BLOCK, SCAN_BLOCK), 0)
           >= lax.broadcasted_iota(jnp.int32, (SCAN_BLOCK, SCAN_BLOCK), 1))
    mid = SCAN_BLOCK // 2
    nt = (((1,), (1,)), ((), ()))
    tn = (((0,), (0,)), ((), ()))

    def head_consts(hh):
        lanes = slice(hh * LANE, (hh + 1) * LANE)
        logits = lbl_ref[:, lanes]
        ex = jnp.exp(logits - jnp.max(logits, axis=0, keepdims=True))
        lb = jnp.sum(ex[0:layer + 1, :], axis=0, keepdims=True) / jnp.sum(ex, axis=0, keepdims=True)
        return lb, 1.0 - lb, convw_ref[0:1, lanes], convw_ref[1:2, lanes], convw_ref[2:3, lanes]

    consts = [head_consts(hh) for hh in range(hb)]

    def prefix8(x):
        sh = 1
        while sh < SUBLANES:
            x = x + jnp.where(sub >= sh, pltpu.roll(x, sh, 0), 0.0)
            sh *= 2
        return x

    def conv_block(hh, lo):
        _, _, w0, w1, w2 = consts[hh]
        rows = pl.ds(lo, SCAN_BLOCK)
        cu = cg_ref[hh, rows, :] * u_ref[hh, rows, :]
        cu_ref[hh, pl.ds(SUBLANES + lo, SCAN_BLOCK), :] = cu
        s1 = cu_ref[hh, pl.ds(SUBLANES - 1 + lo, SCAN_BLOCK), :]
        s2 = cu_ref[hh, pl.ds(SUBLANES - 2 + lo, SCAN_BLOCK), :]
        conv = w0 * s2 + w1 * s1 + w2 * cu
        ya_ref[rows, hh * LANE:(hh + 1) * LANE] = (bg_ref[hh, rows, :] * conv).astype(ya_ref.dtype)

    def block_terms(hh, lo):
        lb, one_m_lb = consts[hh][0:2]
        rows = pl.ds(lo, SCAN_BLOCK)
        f = lb + one_m_lb * jax.nn.sigmoid(fz_ref[hh, rows, :])
        g2 = jnp.log2(f)
        k = 1.0 - f
        q = _silu(q_ref[hh, rows, :])
        v = iv_ref[hh, rows, :].astype(BF16)
        parts = []
        total = None
        for p in range(SCAN_BLOCK // SUBLANES):
            part = prefix8(g2[p * SUBLANES:(p + 1) * SUBLANES, :])
            if total is not None:
                part = part + total
            total = part[SUBLANES - 1:SUBLANES, :]
            parts.append(part)
        lc = jnp.concatenate(parts, axis=0)
        ref = lc[mid - 1:mid, :]
        qd = (q * jnp.exp2(lc - ref)).astype(BF16)
        kd = (k * jnp.exp2(ref - lc)).astype(BF16)
        att = lax.dot_general(qd, kd, nt, preferred_element_type=F32)
        att = jnp.where(tri, att, 0.0).astype(BF16)
        return q, k, v, lc, total, att

    def state_free(hh, lo):
        qa, ka, va, lca, lla, att_a = block_terms(hh, lo)
        qb, kb, vb, lcb, llb, att_b = block_terms(hh, lo + SCAN_BLOCK)
        qe_a = (qa * jnp.exp2(lca)).astype(BF16)
        qe_b = (qb * jnp.exp2(lcb)).astype(BF16)
        kdec_a = (ka * jnp.exp2(lla - lca)).astype(BF16)
        kdec_b = (kb * jnp.exp2(llb - lcb)).astype(BF16)
        cross = lax.dot_general(qe_b, kdec_a, nt, preferred_element_type=F32).astype(BF16)
        qe = jnp.concatenate([qe_a, (qb * jnp.exp2(lcb + lla)).astype(BF16)], axis=0)
        kend = jnp.concatenate([(ka * jnp.exp2(lla + llb - lca)).astype(BF16), kdec_b], axis=0)
        upd = lax.dot_general(jnp.concatenate([va, vb], axis=0), kend, tn,
                              preferred_element_type=F32)
        return (att_a, att_b, cross, va, vb), upd, jnp.exp2(lla + llb), qe

    pair = 2 * SCAN_BLOCK
    pairs = [(hh, j * pair) for hh in range(hb) for j in range(tc // pair)]
    st = [st_ref[hh] for hh in range(hb)]
    pending = {}
    for n in range(len(pairs) + STATE_LAG):
        if n < len(pairs):
            hh, lo = pairs[n]
            conv_block(hh, lo)
            conv_block(hh, lo + SCAN_BLOCK)
            pending[n] = state_free(hh, lo)
        if n >= STATE_LAG:
            hh, lo = pairs[n - STATE_LAG]
            (att_a, att_b, cross, va, vb), upd, decay, qe = pending.pop(n - STATE_LAG)
            o_a = jnp.dot(att_a, va, preferred_element_type=F32)
            o_b = (jnp.dot(att_b, vb, preferred_element_type=F32)
                   + jnp.dot(cross, va, preferred_element_type=F32))
            o = (jnp.concatenate([o_a, o_b], axis=0)
                 + lax.dot_general(qe, st[hh].astype(BF16), nt, preferred_element_type=F32))
            st[hh] = st[hh] * decay + upd
            rows = pl.ds(lo, pair)
            ms = jnp.mean(o * o, axis=-1, keepdims=True)
            yb = o * lax.rsqrt(ms + NORM_EPS) * (hgn * _silu(og_ref[hh, rows, :]))
            yb_ref[rows, hh * LANE:(hh + 1) * LANE] = yb.astype(yb_ref.dtype)
    for hh in range(hb):
        st_ref[hh] = st[hh]
        tail = cu_ref[hh, SUBLANES + tc - (CONV_W - 1):SUBLANES + tc, :]
        cu_ref[hh, SUBLANES - (CONV_W - 1):SUBLANES, :] = tail
        nconv_ref[0, :, hh * LANE:(hh + 1) * LANE] = tail

    @pl.when(c == n_chunks - 1)
    def _fin():
        for hh in range(hb):
            nst_ref[0, hh] = st[hh].T


def _mixer(proj, conv_w, lb_logits, hg_norm, *, row0, nseq, t, tc, hb, n_groups, layer, cache=None, state=None):
    has_cache = cache is not None
    if has_cache != (state is not None):
        raise ValueError("conv cache and recurrent state come together")
    n_chunks = t // tc
    rb0 = row0 // tc
    n_hblk = n_groups // hb

    def pspec(seg):
        return pl.BlockSpec((hb, tc, LANE), lambda b, h, c: (seg * n_hblk + h, rb0 + b * n_chunks + c, 0))

    in_specs = [pspec(s) for s in range(7)]
    in_specs += [
        pl.BlockSpec((CONV_W, hb * LANE), lambda b, h, c: (0, h)),
        pl.BlockSpec((lb_logits.shape[0], hb * LANE), lambda b, h, c: (0, h)),
        pl.BlockSpec((1, LANE), lambda b, h, c: (0, 0)),
    ]
    args = [proj] * 7 + [conv_w, lb_logits, hg_norm]
    if has_cache:
        in_specs += [
            pl.BlockSpec((1, CONV_W - 1, hb * LANE), lambda b, h, c: (b, 0, h)),
            pl.BlockSpec((1, hb, HEAD_DIM, HEAD_DIM), lambda b, h, c: (b, h, 0, 0)),
        ]
        args += [cache, state]
    d_half = n_groups * LANE
    yspec = pl.BlockSpec((tc, hb * LANE), lambda b, h, c: (b * n_chunks + c, h))
    out_specs = [
        yspec, yspec,
        pl.BlockSpec((1, CONV_W - 1, hb * LANE), lambda b, h, c: (b, 0, h)),
        pl.BlockSpec((1, hb, HEAD_DIM, HEAD_DIM), lambda b, h, c: (b, h, 0, 0)),
    ]
    out_shape = [
        jax.ShapeDtypeStruct((nseq * t, d_half), BF16),
        jax.ShapeDtypeStruct((nseq * t, d_half), BF16),
        jax.ShapeDtypeStruct((nseq, CONV_W - 1, d_half), F32),
        jax.ShapeDtypeStruct((nseq, n_groups, HEAD_DIM, HEAD_DIM), F32),
    ]
    kern = functools.partial(_mixer_kernel, tc=tc, hb=hb, n_chunks=n_chunks, has_cache=has_cache,
                             layer=layer)
    return pl.pallas_call(
        kern,
        grid=(nseq, n_hblk, n_chunks),
        in_specs=in_specs,
        out_specs=out_specs,
        out_shape=out_shape,
        scratch_shapes=[pltpu.VMEM((hb, HEAD_DIM, HEAD_DIM), F32),
                        pltpu.VMEM((hb, SUBLANES + tc, LANE), F32)],
        compiler_params=_params(("parallel", "parallel", "arbitrary")),
        name="mixer_cached" if has_cache else "mixer_prompt",
    )(*args)


def _out_proj_kernel(yap_ref, ybp_ref, yas_ref, ybs_ref, w_ref, xp_ref, xs_ref, wn_ref,
                     o_ref, xw_ref, ssq_ref, *, n_p):
    i = pl.program_id(1)

    def emit(ya_ref, yb_ref, x_ref):
        y = jnp.concatenate([ya_ref[...], yb_ref[...]], axis=1)
        x1 = x_ref[...] + jnp.dot(y, w_ref[...], preferred_element_type=F32)
        o_ref[...] = x1
        xw_ref[...] = (x1 * wn_ref[...]).astype(xw_ref.dtype)
        sq = x1 * x1
        acc = sq[:, 0:LANE]
        for g in range(1, sq.shape[1] // LANE):
            acc = acc + sq[:, g * LANE:(g + 1) * LANE]
        ssq_ref[...] = acc

    @pl.when(i < n_p)
    def _():
        emit(yap_ref, ybp_ref, xp_ref)

    @pl.when(i >= n_p)
    def _():
        emit(yas_ref, ybs_ref, xs_ref)


def _out_proj(ya_p, yb_p, ya_s, yb_s, w, xp, xs, w_norm, tm, tn):
    mp, dh = ya_p.shape
    ms = ya_s.shape[0]
    kdim, n = w.shape
    n_p, n_s = mp // tm, ms // tm
    m = mp + ms

    def prow(j, i):
        return jnp.minimum(i, n_p - 1)

    def srow(j, i):
        return jnp.maximum(i - n_p, 0)

    return pl.pallas_call(
        functools.partial(_out_proj_kernel, n_p=n_p),
        grid=(n // tn, n_p + n_s),
        in_specs=[
            pl.BlockSpec((tm, dh), lambda j, i: (prow(j, i), 0)),
            pl.BlockSpec((tm, dh), lambda j, i: (prow(j, i), 0)),
            pl.BlockSpec((tm, dh), lambda j, i: (srow(j, i), 0)),
            pl.BlockSpec((tm, dh), lambda j, i: (srow(j, i), 0)),
            pl.BlockSpec((kdim, tn), lambda j, i: (0, j)),
            pl.BlockSpec((tm, tn), lambda j, i: (prow(j, i), j)),
            pl.BlockSpec((tm, tn), lambda j, i: (srow(j, i), j)),
            pl.BlockSpec((1, tn), lambda j, i: (0, j)),
        ],
        out_specs=[
            pl.BlockSpec((tm, tn), lambda j, i: (i, j)),
            pl.BlockSpec((tm, tn), lambda j, i: (i, j)),
            pl.BlockSpec((tm, LANE), lambda j, i: (i, j)),
        ],
        out_shape=[
            jax.ShapeDtypeStruct((m, n), F32),
            jax.ShapeDtypeStruct((m, n), BF16),
            jax.ShapeDtypeStruct((m, (n // tn) * LANE), F32),
        ],
        compiler_params=_params(("parallel", "parallel")),
        name="out_proj",
    )(ya_p, yb_p, ya_s, yb_s, w, xp, xs, w_norm)


FFN_UP_SUB_ROWS = 256


def _ffn_up_kernel(a_ref, ssq_ref, wg_ref, wu_ref, o_ref, *, n_valid, sub_rows):
    j = pl.program_id(0)

    @pl.when(j < n_valid)
    def _():
        wg = wg_ref[...].astype(BF16)
        wu = wu_ref[...].astype(BF16)
        for r0 in range(0, a_ref.shape[0], sub_rows):
            rows = pl.ds(r0, sub_rows)
            a = a_ref[rows, :]
            r = lax.rsqrt(jnp.sum(ssq_ref[rows, :], axis=-1, keepdims=True) / a.shape[1] + NORM_EPS)
            g = r * jnp.dot(a, wg, preferred_element_type=F32)
            u = r * jnp.dot(a, wu, preferred_element_type=F32)
            o_ref[rows, :] = (_silu(g) * u).astype(o_ref.dtype)

    @pl.when(j >= n_valid)
    def _():
        o_ref[...] = jnp.zeros_like(o_ref)


def _ffn_up(xw, ssq, wg, wu, tm, tn, f_pad):
    m, k = xw.shape
    n_valid = wg.shape[1] // tn
    wspec = pl.BlockSpec((k, tn), lambda j, i: (0, jnp.minimum(j, n_valid - 1)))
    return pl.pallas_call(
        functools.partial(_ffn_up_kernel, n_valid=n_valid,
                          sub_rows=_largest_tile(tm, FFN_UP_SUB_ROWS, BF16_ROWS)),
        grid=(f_pad // tn, m // tm),
        in_specs=[
            pl.BlockSpec((tm, k), lambda j, i: (i, 0)),
            pl.BlockSpec((tm, ssq.shape[1]), lambda j, i: (i, 0)),
            wspec, wspec,
        ],
        out_specs=pl.BlockSpec((tm, tn), lambda j, i: (i, j)),
        out_shape=jax.ShapeDtypeStruct((m, f_pad), BF16),
        compiler_params=_params(("parallel", "parallel")),
        name="ffn_up",
    )(xw, ssq, wg, wu)


def _ffn_down_kernel(a_ref, w_ref, x_ref, o_ref):
    o_ref[...] = x_ref[...] + jnp.dot(a_ref[...], w_ref[...], preferred_element_type=F32)


def _ffn_down(act, w, x1, tm, tn):
    m, kf = act.shape
    n = w.shape[1]
    return pl.pallas_call(
        _ffn_down_kernel,
        grid=(n // tn, m // tm),
        in_specs=[
            pl.BlockSpec((tm, kf), lambda j, i: (i, 0)),
            pl.BlockSpec((kf, tn), lambda j, i: (0, j)),
            pl.BlockSpec((tm, tn), lambda j, i: (i, j)),
        ],
        out_specs=pl.BlockSpec((tm, tn), lambda j, i: (i, j)),
        out_shape=jax.ShapeDtypeStruct((m, n), F32),
        compiler_params=_params(("parallel", "parallel")),
        name="ffn_down",
    )(act, w, x1)


def _mixer_tiles(t, n_groups):
    tc = _largest_tile(t, MIXER_BLOCKS_PER_STEP * SCAN_BLOCK, 2 * SCAN_BLOCK)
    hb = 1
    while hb < n_groups and 2 * hb * tc <= MIXER_BLOCKS_PER_STEP * SCAN_BLOCK and n_groups % (2 * hb) == 0:
        hb *= 2
    return dict(tc=tc, hb=hb)


def _tiles(mp, ms, tp, ts, n_groups, d, n_in, d_ff, f_pad):
    m = mp + ms
    tr = _largest_tile(_gcd(mp, ms), 512, BF16_ROWS)
    return dict(
        tr=tr,
        tm=_largest_tile(m, 1536, BF16_ROWS),
        tn_in=_largest_tile(n_in, 512, LANE),
        tn_out=_largest_tile(d, 1024, LANE),
        tn_up=_largest_tile(_gcd(d_ff, f_pad), 256, LANE),
        tm_down=_largest_tile(m, 528, BF16_ROWS),
        tn_down=_largest_tile(d, 512, LANE),
        mix_p=_mixer_tiles(tp, n_groups),
        mix_s=_mixer_tiles(ts, n_groups),
    )


def kernel(x_prompt, x_sample, cache_conv, state_hgrn, norm_mix, w_in, conv_w, lb_logits, hg_norm,
           w_out, norm_ffn, w_gate, w_up, w_down, norm_final):
    bp, tp, d = x_prompt.shape
    bs, ts, _ = x_sample.shape
    depth = w_in.shape[0]
    if depth != 1:
        raise NotImplementedError("one layer")
    layer = 0
    n_in = w_in.shape[2]
    d_ff = w_gate.shape[2]
    d_half = conv_w.shape[2]
    n_groups = d_half // LANE
    if n_in != 7 * d_half or state_hgrn.shape[2] != n_groups or w_out.shape[1] != 2 * d_half:
        raise ValueError("unexpected layer geometry")
    mp, ms = bp * tp, bs * ts
    m = mp + ms
    f_pad = _round_up(d_ff, 1024)
    t = _tiles(mp, ms, tp, ts, n_groups, d, n_in, d_ff, f_pad)

    xp = x_prompt.reshape(mp, d)
    xs = x_sample.reshape(ms, d)
    h = _norm_two_src(xp, xs, norm_mix[layer].reshape(1, d), t["tr"])
    proj, (w_out_b, w_down_b) = _in_proj(h, w_in[layer], t["tm"], t["tn_in"],
                                         casts=[(w_out[layer], w_out.shape[1]), (w_down[layer], f_pad)])

    mix_args = dict(n_groups=n_groups, layer=layer)
    hgn = hg_norm[layer].reshape(1, HEAD_DIM)
    ya_p, yb_p, conv_p, hg_p = _mixer(proj, conv_w[layer], lb_logits, hgn, row0=0, nseq=bp, t=tp,
                                      **t["mix_p"], **mix_args)
    ya_s, yb_s, conv_s, hg_s = _mixer(proj, conv_w[layer], lb_logits, hgn, row0=mp, nseq=bs, t=ts,
                                      cache=cache_conv[layer], state=state_hgrn[layer],
                                      **t["mix_s"], **mix_args)

    x1, x1w, ssq = _out_proj(ya_p, yb_p, ya_s, yb_s, w_out_b, xp, xs, norm_ffn[layer].reshape(1, d),
                             t["tr"], t["tn_out"])
    act = _ffn_up(x1w, ssq, w_gate[layer], w_up[layer], t["tm"], t["tn_up"], f_pad)
    x2 = _ffn_down(act, w_down_b, x1, t["tm_down"], t["tn_down"])
    y_p, y_s = _norm_two_sink(x2, norm_final.reshape(1, d), mp, t["tr"])

    return (y_p.reshape(bp, tp, d), y_s.reshape(bs, ts, d),
            conv_p[None], hg_p[None].astype(x_prompt.dtype),
            conv_s[None].astype(cache_conv.dtype), hg_s[None].astype(state_hgrn.dtype))
```

```python
import functools

import jax
import jax.numpy as jnp
from jax import lax
from jax.experimental import pallas as pl
from jax.experimental.pallas import tpu as pltpu

LANE = 128
SUBLANES = 8
BF16_ROWS = 16
SCAN_BLOCK = 16
HEAD_DIM = 128
CONV_W = 3
NORM_EPS = 1e-6
V7X_VMEM_BYTES = 64 * 1024 * 1024
VMEM_LIMIT = V7X_VMEM_BYTES - 6 * 1024 * 1024

F32 = jnp.float32
BF16 = jnp.bfloat16


def _largest_tile(n, cap, quantum):
    best = None
    for t in range(quantum, min(n, cap) + 1, quantum):
        if n % t == 0:
            best = t
    if best is None:
        raise ValueError(f"no tile for n={n} cap={cap} quantum={quantum}")
    return best


def _gcd(a, b):
    while b:
        a, b = b, a % b
    return a


def _params(sem):
    return pltpu.CompilerParams(dimension_semantics=sem, vmem_limit_bytes=VMEM_LIMIT)


def _rms_rows(x, w):
    ms = jnp.mean(x * x, axis=-1, keepdims=True)
    return x * lax.rsqrt(ms + NORM_EPS) * w


def _silu(x):
    return x * jax.nn.sigmoid(x)


def _norm_two_src_kernel(xp_ref, xs_ref, w_ref, o_ref, *, n_p):
    i = pl.program_id(0)

    @pl.when(i < n_p)
    def _():
        o_ref[...] = _rms_rows(xp_ref[...], w_ref[...]).astype(o_ref.dtype)

    @pl.when(i >= n_p)
    def _():
        o_ref[...] = _rms_rows(xs_ref[...], w_ref[...]).astype(o_ref.dtype)


def _norm_two_src(xp, xs, w, tr):
    mp, d = xp.shape
    ms = xs.shape[0]
    n_p, n_s = mp // tr, ms // tr
    return pl.pallas_call(
        functools.partial(_norm_two_src_kernel, n_p=n_p),
        grid=(n_p + n_s,),
        in_specs=[
            pl.BlockSpec((tr, d), lambda i: (jnp.minimum(i, n_p - 1), 0)),
            pl.BlockSpec((tr, d), lambda i: (jnp.maximum(i - n_p, 0), 0)),
            pl.BlockSpec((1, d), lambda i: (0, 0)),
        ],
        out_specs=pl.BlockSpec((tr, d), lambda i: (i, 0)),
        out_shape=jax.ShapeDtypeStruct((mp + ms, d), BF16),
        compiler_params=_params(("parallel",)),
        name="norm1",
    )(xp, xs, w)


def _norm_two_sink_kernel(x_ref, w_ref, op_ref, os_ref, *, n_p):
    i = pl.program_id(0)

    @pl.when(i < n_p)
    def _():
        op_ref[...] = _rms_rows(x_ref[...], w_ref[...])

    @pl.when(i >= n_p)
    def _():
        os_ref[...] = _rms_rows(x_ref[...], w_ref[...])


def _norm_two_sink(x, w, mp, tr):
    m, d = x.shape
    ms = m - mp
    n_p, n_s = mp // tr, ms // tr
    return pl.pallas_call(
        functools.partial(_norm_two_sink_kernel, n_p=n_p),
        grid=(n_p + n_s,),
        in_specs=[pl.BlockSpec((tr, d), lambda i: (i, 0)), pl.BlockSpec((1, d), lambda i: (0, 0))],
        out_specs=[
            pl.BlockSpec((tr, d), lambda i: (jnp.minimum(i, n_p - 1), 0)),
            pl.BlockSpec((tr, d), lambda i: (jnp.maximum(i - n_p, 0), 0)),
        ],
        out_shape=[jax.ShapeDtypeStruct((mp, d), F32), jax.ShapeDtypeStruct((ms, d), F32)],
        compiler_params=_params(("arbitrary",)),
        name="final_norm",
    )(x, w)


def _cast_rows(r, n_steps):
    for rows in range(BF16_ROWS, r + 1, BF16_ROWS):
        if r % rows == 0 and r // rows <= n_steps:
            return rows
    raise ValueError(f"cannot cast {r} rows in {n_steps} grid steps")


def _in_proj_kernel(*refs, n_cast):
    a_ref, b_ref = refs[0:2]
    cast_in = refs[2:2 + n_cast]
    o_ref = refs[2 + n_cast]
    cast_out = refs[3 + n_cast:3 + 2 * n_cast]
    for src, dst in zip(cast_in, cast_out):
        dst[...] = src[...].astype(dst.dtype)

    acc = jnp.dot(a_ref[...], b_ref[...].astype(BF16), preferred_element_type=F32)
    for g in range(o_ref.shape[0]):
        o_ref[g] = acc[:, g * LANE:(g + 1) * LANE]


def _in_proj(h, w, tm, tn, casts):
    m, k = h.shape
    n = w.shape[1]
    ni, nj = m // tm, n // tn
    in_specs = [pl.BlockSpec((tm, k), lambda i, j: (i, 0)), pl.BlockSpec((k, tn), lambda i, j: (0, j))]
    out_specs = [pl.BlockSpec((tn // LANE, tm, LANE), lambda i, j: (j, i, 0))]
    out_shape = [jax.ShapeDtypeStruct((n // LANE, m, LANE), F32)]
    for arr in casts:
        r, c = arr.shape
        rows = _cast_rows(r, ni * nj)
        spec = pl.BlockSpec((rows, c), lambda i, j, last=r // rows - 1: (jnp.minimum(i * nj + j, last), 0))
        in_specs.append(spec)
        out_specs.append(spec)
        out_shape.append(jax.ShapeDtypeStruct((r, c), BF16))
    outs = pl.pallas_call(
        functools.partial(_in_proj_kernel, n_cast=len(casts)),
        grid=(ni, nj),
        in_specs=in_specs,
        out_specs=out_specs,
        out_shape=out_shape,
        compiler_params=_params(("arbitrary", "arbitrary")),
        name="in_proj",
    )(h, w, *casts)
    return outs[0], outs[1:]


STATE_LAG = 2
MIXER_BLOCKS_PER_STEP = 256


def _mixer_kernel(*refs, tc, hb, n_chunks, has_cache, layer):
    if has_cache:
        (bg_ref, cg_ref, u_ref, q_ref, fz_ref, iv_ref, og_ref, convw_ref, lbl_ref, hgn_ref,
         cbuf_ref, s0_ref, ya_ref, yb_ref, nconv_ref, nst_ref, st_ref, cu_ref) = refs
    else:
        (bg_ref, cg_ref, u_ref, q_ref, fz_ref, iv_ref, og_ref, convw_ref, lbl_ref, hgn_ref,
         ya_ref, yb_ref, nconv_ref, nst_ref, st_ref, cu_ref) = refs
    c = pl.program_id(2)

    @pl.when(c == 0)
    def _init():
        for hh in range(hb):
            if has_cache:
                st_ref[hh] = s0_ref[0, hh].T
                cu_ref[hh, SUBLANES - (CONV_W - 1):SUBLANES, :] = cbuf_ref[0, :, hh * LANE:(hh + 1) * LANE]
            else:
                st_ref[hh] = jnp.zeros((HEAD_DIM, HEAD_DIM), F32)
                cu_ref[hh, 0:SUBLANES, :] = jnp.zeros((SUBLANES, LANE), F32)

    hgn = hgn_ref[...]
    sub = lax.broadcasted_iota(jnp.int32, (SUBLANES, LANE), 0)
    tri = (lax.broadcasted_iota(jnp.int32, (SCAN_BLOCK, SCAN_BLOCK), 0)
           >= lax.broadcasted_iota(jnp.int32, (SCAN_BLOCK, SCAN_BLOCK), 1))
    mid = SCAN_BLOCK // 2
    nt = (((1,), (1,)), ((), ()))
    tn = (((0,), (0,)), ((), ()))

    def head_consts(hh):
        lanes = slice(hh * LANE, (hh + 1) * LANE)
        logits = lbl_ref[:, lanes]
        ex = jnp.exp(logits - jnp.max(logits, axis=0, keepdims=True))
        lb = jnp.sum(ex[0:layer + 1, :], axis=0, keepdims=True) / jnp.sum(ex, axis=0, keepdims=True)
        return lb, 1.0 - lb, convw_ref[0:1, lanes], convw_ref[1:2, lanes], convw_ref[2:3, lanes]

    consts = [head_consts(hh) for hh in range(hb)]

    def prefix8(x):
        sh = 1
        while sh < SUBLANES:
            x = x + jnp.where(sub >= sh, pltpu.roll(x, sh, 0), 0.0)
            sh *= 2
        return x

    def conv_block(hh, lo):
        _, _, w0, w1, w2 = consts[hh]
        rows = pl.ds(lo, SCAN_BLOCK)
        cu = cg_ref[hh, rows, :] * u_ref[hh, rows, :]
        cu_ref[hh, pl.ds(SUBLANES + lo, SCAN_BLOCK), :] = cu
        s1 = cu_ref[hh, pl.ds(SUBLANES - 1 + lo, SCAN_BLOCK), :]
        s2 = cu_ref[hh, pl.ds(SUBLANES - 2 + lo, SCAN_BLOCK), :]
        conv = w0 * s2 + w1 * s1 + w2 * cu
        ya_ref[rows, hh * LANE:(hh + 1) * LANE] = (bg_ref[hh, rows, :] * conv).astype(ya_ref.dtype)

    def block_terms(hh, lo):
        lb, one_m_lb = consts[hh][0:2]
        rows = pl.ds(lo, SCAN_BLOCK)
        f = lb + one_m_lb * jax.nn.sigmoid(fz_ref[hh, rows, :])
        g2 = jnp.log2(f)
        k = 1.0 - f
        q = _silu(q_ref[hh, rows, :])
        v = iv_ref[hh, rows, :].astype(BF16)
        parts = []
        total = None
        for p in range(SCAN_BLOCK // SUBLANES):
            part = prefix8(g2[p * SUBLANES:(p + 1) * SUBLANES, :])
            if total is not None:
                part = part + total
            total = part[SUBLANES - 1:SUBLANES, :]
            parts.append(part)
        lc = jnp.concatenate(parts, axis=0)
        ref = lc[mid - 1:mid, :]
        qd = (q * jnp.exp2(lc - ref)).astype(BF16)
        kd = (k * jnp.exp2(ref - lc)).astype(BF16)
        att = lax.dot_general(qd, kd, nt, preferred_element_type=F32)
        att = jnp.where(tri, att, 0.0).astype(BF16)
        return q, k, v, lc, total, att

    def state_free(hh, lo):
        qa, ka, va, lca, lla, att_a = block_terms(hh, lo)
        qb, kb, vb, lcb, llb, att_b = block_terms(hh, lo + SCAN_BLOCK)
        qe_a = (qa * jnp.exp2(lca)).astype(BF16)
        qe_b = (qb * jnp.exp2(lcb)).astype(BF16)
        kdec_a = (ka * jnp.exp2(lla - lca)).astype(BF16)
        kdec_b = (kb * jnp.exp2(llb - lcb)).astype(BF16)
        cross = lax.dot_general(qe_b, kdec_a, nt, preferred_element_type=F32).astype(BF16)
        qe = jnp.concatenate([qe_a, (qb * jnp.exp2(lcb + lla)).astype(BF16)], axis=0)
        kend = jnp.concatenate([(ka * jnp.exp2(lla + llb - lca)).astype(BF16), kdec_b], axis=0)
        upd = lax.dot_general(jnp.concatenate([va, vb], axis=0), kend, tn,
                              preferred_element_type=F32)
        return (att_a, att_b, cross, va, vb), upd, jnp.exp2(lla + llb), qe

    pair = 2 * SCAN_BLOCK
    pairs = [(hh, j * pair) for hh in range(hb) for j in range(tc // pair)]
    st = [st_ref[hh] for hh in range(hb)]
    pending = {}
    for n in range(len(pairs) + STATE_LAG):
        if n < len(pairs):
            hh, lo = pairs[n]
            conv_block(hh, lo)
            conv_block(hh, lo + SCAN_BLOCK)
            pending[n] = state_free(hh, lo)
        if n >= STATE_LAG:
            hh, lo = pairs[n - STATE_LAG]
            (att_a, att_b, cross, va, vb), upd, decay, qe = pending.pop(n - STATE_LAG)
            o_a = jnp.dot(att_a, va, preferred_element_type=F32)
            o_b = (jnp.dot(att_b, vb, preferred_element_type=F32)
                   + jnp.dot(cross, va, preferred_element_type=F32))
            o = (jnp.concatenate([o_a, o_b], axis=0)
                 + lax.dot_general(qe, st[hh].astype(BF16), nt, preferred_element_type=F32))
            st[hh] = st[hh] * decay + upd
            rows = pl.ds(lo, pair)
            ms = jnp.mean(o * o, axis=-1, keepdims=True)
            yb = o * lax.rsqrt(ms + NORM_EPS) * (hgn * _silu(og_ref[hh, rows, :]))
            yb_ref[rows, hh * LANE:(hh + 1) * LANE] = yb.astype(yb_ref.dtype)
    for hh in range(hb):
        st_ref[hh] = st[hh]
        tail = cu_ref[hh, SUBLANES + tc - (CONV_W - 1):SUBLANES + tc, :]
        cu_ref[hh, SUBLANES - (CONV_W - 1):SUBLANES, :] = tail
        nconv_ref[0, :, hh * LANE:(hh + 1) * LANE] = tail

    @pl.when(c == n_chunks - 1)
    def _fin():
        for hh in range(hb):
            nst_ref[0, hh] = st[hh].T


def _mixer(proj, conv_w, lb_logits, hg_norm, *, row0, nseq, t, tc, hb, n_groups, layer, cache=None, state=None):
    has_cache = cache is not None
    if has_cache != (state is not None):
        raise ValueError("conv cache and recurrent state come together")
    n_chunks = t // tc
    rb0 = row0 // tc
    n_hblk = n_groups // hb

    def pspec(seg):
        return pl.BlockSpec((hb, tc, LANE), lambda b, h, c: (seg * n_hblk + h, rb0 + b * n_chunks + c, 0))

    in_specs = [pspec(s) for s in range(7)]
    in_specs += [
        pl.BlockSpec((CONV_W, hb * LANE), lambda b, h, c: (0, h)),
        pl.BlockSpec((lb_logits.shape[0], hb * LANE), lambda b, h, c: (0, h)),
        pl.BlockSpec((1, LANE), lambda b, h, c: (0, 0)),
    ]
    args = [proj] * 7 + [conv_w, lb_logits, hg_norm]
    if has_cache:
        in_specs += [
            pl.BlockSpec((1, CONV_W - 1, hb * LANE), lambda b, h, c: (b, 0, h)),
            pl.BlockSpec((1, hb, HEAD_DIM, HEAD_DIM), lambda b, h, c: (b, h, 0, 0)),
        ]
        args += [cache, state]
    d_half = n_groups * LANE
    yspec = pl.BlockSpec((tc, hb * LANE), lambda b, h, c: (b * n_chunks + c, h))
    out_specs = [
        yspec, yspec,
        pl.BlockSpec((1, CONV_W - 1, hb * LANE), lambda b, h, c: (b, 0, h)),
        pl.BlockSpec((1, hb, HEAD_DIM, HEAD_DIM), lambda b, h, c: (b, h, 0, 0)),
    ]
    out_shape = [
        jax.ShapeDtypeStruct((nseq * t, d_half), BF16),
        jax.ShapeDtypeStruct((nseq * t, d_half), BF16),
        jax.ShapeDtypeStruct((nseq, CONV_W - 1, d_half), F32),
        jax.ShapeDtypeStruct((nseq, n_groups, HEAD_DIM, HEAD_DIM), F32),
    ]
    kern = functools.partial(_mixer_kernel, tc=tc, hb=hb, n_chunks=n_chunks, has_cache=has_cache,
                             layer=layer)
    return pl.pallas_call(
        kern,
        grid=(nseq, n_hblk, n_chunks),
        in_specs=in_specs,
        out_specs=out_specs,
        out_shape=out_shape,
        scratch_shapes=[pltpu.VMEM((hb, HEAD_DIM, HEAD_DIM), F32),
                        pltpu.VMEM((hb, SUBLANES + tc, LANE), F32)],
        compiler_params=_params(("parallel", "parallel", "arbitrary")),
        name="mixer_cached" if has_cache else "mixer_prompt",
    )(*args)


def _out_proj_kernel(yap_ref, ybp_ref, yas_ref, ybs_ref, w_ref, xp_ref, xs_ref, wn_ref,
                     o_ref, xw_ref, ssq_ref, *, n_p):
    i = pl.program_id(1)

    def emit(ya_ref, yb_ref, x_ref):
        y = jnp.concatenate([ya_ref[...], yb_ref[...]], axis=1)
        x1 = x_ref[...] + jnp.dot(y, w_ref[...], preferred_element_type=F32)
        o_ref[...] = x1
        xw_ref[...] = (x1 * wn_ref[...]).astype(xw_ref.dtype)
        sq = x1 * x1
        acc = sq[:, 0:LANE]
        for g in range(1, sq.shape[1] // LANE):
            acc = acc + sq[:, g * LANE:(g + 1) * LANE]
        ssq_ref[...] = acc

    @pl.when(i < n_p)
    def _():
        emit(yap_ref, ybp_ref, xp_ref)

    @pl.when(i >= n_p)
    def _():
        emit(yas_ref, ybs_ref, xs_ref)


def _out_proj(ya_p, yb_p, ya_s, yb_s, w, xp, xs, w_norm, tm, tn):
    mp, dh = ya_p.shape
    ms = ya_s.shape[0]
    kdim, n = w.shape
    n_p, n_s = mp // tm, ms // tm
    m = mp + ms

    def prow(j, i):
        return jnp.minimum(i, n_p - 1)

    def srow(j, i):
        return jnp.maximum(i - n_p, 0)

    return pl.pallas_call(
        functools.partial(_out_proj_kernel, n_p=n_p),
        grid=(n // tn, n_p + n_s),
        in_specs=[
            pl.BlockSpec((tm, dh), lambda j, i: (prow(j, i), 0)),
            pl.BlockSpec((tm, dh), lambda j, i: (prow(j, i), 0)),
            pl.BlockSpec((tm, dh), lambda j, i: (srow(j, i), 0)),
            pl.BlockSpec((tm, dh), lambda j, i: (srow(j, i), 0)),
            pl.BlockSpec((kdim, tn), lambda j, i: (0, j)),
            pl.BlockSpec((tm, tn), lambda j, i: (prow(j, i), j)),
            pl.BlockSpec((tm, tn), lambda j, i: (srow(j, i), j)),
            pl.BlockSpec((1, tn), lambda j, i: (0, j)),
        ],
        out_specs=[
            pl.BlockSpec((tm, tn), lambda j, i: (i, j)),
            pl.BlockSpec((tm, tn), lambda j, i: (i, j)),
            pl.BlockSpec((tm, LANE), lambda j, i: (i, j)),
        ],
        out_shape=[
            jax.ShapeDtypeStruct((m, n), F32),
            jax.ShapeDtypeStruct((m, n), BF16),
            jax.ShapeDtypeStruct((m, (n // tn) * LANE), F32),
        ],
        compiler_params=_params(("parallel", "parallel")),
        name="out_proj",
    )(ya_p, yb_p, ya_s, yb_s, w, xp, xs, w_norm)


FFN_UP_SUB_ROWS = 256


def _ffn_up_kernel(a_ref, ssq_ref, wg_ref, wu_ref, o_ref, *, sub_rows):
    wg = wg_ref[...].astype(BF16)
    wu = wu_ref[...].astype(BF16)
    for r0 in range(0, a_ref.shape[0], sub_rows):
        rows = pl.ds(r0, sub_rows)
        a = a_ref[rows, :]
        r = lax.rsqrt(jnp.sum(ssq_ref[rows, :], axis=-1, keepdims=True) / a.shape[1] + NORM_EPS)
        g = r * jnp.dot(a, wg, preferred_element_type=F32)
        u = r * jnp.dot(a, wu, preferred_element_type=F32)
        o_ref[rows, :] = (_silu(g) * u).astype(o_ref.dtype)


def _ffn_up(xw, ssq, wg, wu, tm, tn):
    m, k = xw.shape
    n = wg.shape[1]
    wspec = pl.BlockSpec((k, tn), lambda j, i: (0, j))
    return pl.pallas_call(
        functools.partial(_ffn_up_kernel, sub_rows=_largest_tile(tm, FFN_UP_SUB_ROWS, BF16_ROWS)),
        grid=(n // tn, m // tm),
        in_specs=[
            pl.BlockSpec((tm, k), lambda j, i: (i, 0)),
            pl.BlockSpec((tm, ssq.shape[1]), lambda j, i: (i, 0)),
            wspec, wspec,
        ],
        out_specs=pl.BlockSpec((tm, tn), lambda j, i: (i, j)),
        out_shape=jax.ShapeDtypeStruct((m, n), BF16),
        compiler_params=_params(("parallel", "parallel")),
        name="ffn_up",
    )(xw, ssq, wg, wu)


def _ffn_down_kernel(a_ref, w_ref, x_ref, o_ref):
    o_ref[...] = x_ref[...] + jnp.dot(a_ref[...], w_ref[...], preferred_element_type=F32)


def _ffn_down(act, w, x1, tm, tn):
    m, kf = act.shape
    n = w.shape[1]
    return pl.pallas_call(
        _ffn_down_kernel,
        grid=(n // tn, m // tm),
        in_specs=[
            pl.BlockSpec((tm, kf), lambda j, i: (i, 0)),
            pl.BlockSpec((kf, tn), lambda j, i: (0, j)),
            pl.BlockSpec((tm, tn), lambda j, i: (i, j)),
        ],
        out_specs=pl.BlockSpec((tm, tn), lambda j, i: (i, j)),
        out_shape=jax.ShapeDtypeStruct((m, n), F32),
        compiler_params=_params(("parallel", "parallel")),
        name="ffn_down",
    )(act, w, x1)


def _mixer_tiles(t, n_groups):
    tc = _largest_tile(t, MIXER_BLOCKS_PER_STEP * SCAN_BLOCK, 2 * SCAN_BLOCK)
    hb = 1
    while hb < n_groups and 2 * hb * tc <= MIXER_BLOCKS_PER_STEP * SCAN_BLOCK and n_groups % (2 * hb) == 0:
        hb *= 2
    return dict(tc=tc, hb=hb)


def _tiles(mp, ms, tp, ts, n_groups, d, n_in, d_ff):
    m = mp + ms
    tr = _largest_tile(_gcd(mp, ms), 512, BF16_ROWS)
    return dict(
        tr=tr,
        tm=_largest_tile(m, 1536, BF16_ROWS),
        tn_in=_largest_tile(n_in, 512, LANE),
        tn_out=_largest_tile(d, 1024, LANE),
        tn_up=_largest_tile(d_ff, 256, LANE),
        tm_down=_largest_tile(m, 528, BF16_ROWS),
        tn_down=_largest_tile(d, 512, LANE),
        mix_p=_mixer_tiles(tp, n_groups),
        mix_s=_mixer_tiles(ts, n_groups),
    )


def kernel(x_prompt, x_sample, cache_conv, state_hgrn, norm_mix, w_in, conv_w, lb_logits, hg_norm,
           w_out, norm_ffn, w_gate, w_up, w_down, norm_final):
    bp, tp, d = x_prompt.shape
    bs, ts, _ = x_sample.shape
    depth = w_in.shape[0]
    if depth != 1:
        raise NotImplementedError("one layer")
    layer = 0
    n_in = w_in.shape[2]
    d_ff = w_gate.shape[2]
    d_half = conv_w.shape[2]
    n_groups = d_half // LANE
    if n_in != 7 * d_half or state_hgrn.shape[2] != n_groups or w_out.shape[1] != 2 * d_half:
        raise ValueError("unexpected layer geometry")
    mp, ms = bp * tp, bs * ts
    m = mp + ms
    t = _tiles(mp, ms, tp, ts, n_groups, d, n_in, d_ff)

    xp = x_prompt.reshape(mp, d)
    xs = x_sample.reshape(ms, d)
    h = _norm_two_src(xp, xs, norm_mix[layer].reshape(1, d), t["tr"])
    proj, (w_out_b, w_down_b) = _in_proj(h, w_in[layer], t["tm"], t["tn_in"],
                                         casts=[w_out[layer], w_down[layer]])

    mix_args = dict(n_groups=n_groups, layer=layer)
    hgn = hg_norm[layer].reshape(1, HEAD_DIM)
    ya_p, yb_p, conv_p, hg_p = _mixer(proj, conv_w[layer], lb_logits, hgn, row0=0, nseq=bp, t=tp,
                                      **t["mix_p"], **mix_args)
    ya_s, yb_s, conv_s, hg_s = _mixer(proj, conv_w[layer], lb_logits, hgn, row0=mp, nseq=bs, t=ts,
                                      cache=cache_conv[layer], state=state_hgrn[layer],
                                      **t["mix_s"], **mix_args)

    x1, x1w, ssq = _out_proj(ya_p, yb_p, ya_s, yb_s, w_out_b, xp, xs, norm_ffn[layer].reshape(1, d),
                             t["tr"], t["tn_out"])
    act = _ffn_up(x1w, ssq, w_gate[layer], w_up[layer], t["tm"], t["tn_up"])
    x2 = _ffn_down(act, w_down_b, x1, t["tm_down"], t["tn_down"])
    y_p, y_s = _norm_two_sink(x2, norm_final.reshape(1, d), mp, t["tr"])

    return (y_p.reshape(bp, tp, d), y_s.reshape(bs, ts, d),
            conv_p[None], hg_p[None].astype(x_prompt.dtype),
            conv_s[None].astype(cache_conv.dtype), hg_s[None].astype(state_hgrn.dtype))
```

```python
import functools

import jax
import jax.numpy as jnp
from jax import lax
from jax.experimental import pallas as pl
from jax.experimental.pallas import tpu as pltpu

LANE = 128
SUBLANES = 8
BF16_ROWS = 16
SCAN_BLOCK = 16
HEAD_DIM = 128
CONV_W = 3
NORM_EPS = 1e-6
V7X_VMEM_BYTES = 64 * 1024 * 1024
VMEM_LIMIT = V7X_VMEM_BYTES - 6 * 1024 * 1024

F32 = jnp.float32
BF16 = jnp.bfloat16


def _largest_tile(n, cap, quantum):
    best = None
    for t in range(quantum, min(n, cap) + 1, quantum):
        if n % t == 0:
            best = t
    if best is None:
        raise ValueError(f"no tile for n={n} cap={cap} quantum={quantum}")
    return best


def _gcd(a, b):
    while b:
        a, b = b, a % b
    return a


def _params(sem):
    return pltpu.CompilerParams(dimension_semantics=sem, vmem_limit_bytes=VMEM_LIMIT)


def _rms_rows(x, w):
    ms = jnp.mean(x * x, axis=-1, keepdims=True)
    return x * lax.rsqrt(ms + NORM_EPS) * w


def _silu(x):
    return x * jax.nn.sigmoid(x)


def _norm_two_src_kernel(xp_ref, xs_ref, w_ref, o_ref, *, n_p):
    i = pl.program_id(0)

    @pl.when(i < n_p)
    def _():
        o_ref[...] = _rms_rows(xp_ref[...], w_ref[...]).astype(o_ref.dtype)

    @pl.when(i >= n_p)
    def _():
        o_ref[...] = _rms_rows(xs_ref[...], w_ref[...]).astype(o_ref.dtype)


def _norm_two_src(xp, xs, w, tr):
    mp, d = xp.shape
    ms = xs.shape[0]
    n_p, n_s = mp // tr, ms // tr
    return pl.pallas_call(
        functools.partial(_norm_two_src_kernel, n_p=n_p),
        grid=(n_p + n_s,),
        in_specs=[
            pl.BlockSpec((tr, d), lambda i: (jnp.minimum(i, n_p - 1), 0)),
            pl.BlockSpec((tr, d), lambda i: (jnp.maximum(i - n_p, 0), 0)),
            pl.BlockSpec((1, d), lambda i: (0, 0)),
        ],
        out_specs=pl.BlockSpec((tr, d), lambda i: (i, 0)),
        out_shape=jax.ShapeDtypeStruct((mp + ms, d), BF16),
        compiler_params=_params(("parallel",)),
        name="norm1",
    )(xp, xs, w)


def _norm_two_sink_kernel(x_ref, w_ref, op_ref, os_ref, *, n_p):
    i = pl.program_id(0)

    @pl.when(i < n_p)
    def _():
        op_ref[...] = _rms_rows(x_ref[...], w_ref[...])

    @pl.when(i >= n_p)
    def _():
        os_ref[...] = _rms_rows(x_ref[...], w_ref[...])


def _norm_two_sink(x, w, mp, tr):
    m, d = x.shape
    ms = m - mp
    n_p, n_s = mp // tr, ms // tr
    return pl.pallas_call(
        functools.partial(_norm_two_sink_kernel, n_p=n_p),
        grid=(n_p + n_s,),
        in_specs=[pl.BlockSpec((tr, d), lambda i: (i, 0)), pl.BlockSpec((1, d), lambda i: (0, 0))],
        out_specs=[
            pl.BlockSpec((tr, d), lambda i: (jnp.minimum(i, n_p - 1), 0)),
            pl.BlockSpec((tr, d), lambda i: (jnp.maximum(i - n_p, 0), 0)),
        ],
        out_shape=[jax.ShapeDtypeStruct((mp, d), F32), jax.ShapeDtypeStruct((ms, d), F32)],
        compiler_params=_params(("arbitrary",)),
        name="final_norm",
    )(x, w)


def _cast_rows(r, n_steps):
    for rows in range(BF16_ROWS, r + 1, BF16_ROWS):
        if r % rows == 0 and r // rows <= n_steps:
            return rows
    raise ValueError(f"cannot cast {r} rows in {n_steps} grid steps")


def _in_proj_kernel(*refs, n_cast):
    a_ref, b_ref = refs[0:2]
    cast_in = refs[2:2 + n_cast]
    o_ref = refs[2 + n_cast]
    cast_out = refs[3 + n_cast:3 + 2 * n_cast]
    for src, dst in zip(cast_in, cast_out):
        dst[...] = src[...].astype(dst.dtype)

    acc = jnp.dot(a_ref[...], b_ref[...].astype(BF16), preferred_element_type=F32)
    for g in range(o_ref.shape[0]):
        o_ref[g, 0] = acc[:, g * LANE:(g + 1) * LANE]


def _in_proj(h, w, tm, tn, n_groups, casts):
    m, k = h.shape
    n = w.shape[1]
    ni, nj = m // tm, n // tn
    gpt = tn // LANE
    tps = n_groups // gpt
    in_specs = [pl.BlockSpec((tm, k), lambda i, j: (i, 0)), pl.BlockSpec((k, tn), lambda i, j: (0, j))]
    out_specs = [pl.BlockSpec((gpt, 1, tm, LANE), lambda i, j: (j % tps, j // tps, i, 0))]
    out_shape = [jax.ShapeDtypeStruct((n_groups, n // (n_groups * LANE), m, LANE), F32)]
    for arr in casts:
        r, c = arr.shape
        rows = _cast_rows(r, ni * nj)
        spec = pl.BlockSpec((rows, c), lambda i, j, last=r // rows - 1: (jnp.minimum(i * nj + j, last), 0))
        in_specs.append(spec)
        out_specs.append(spec)
        out_shape.append(jax.ShapeDtypeStruct((r, c), BF16))
    outs = pl.pallas_call(
        functools.partial(_in_proj_kernel, n_cast=len(casts)),
        grid=(ni, nj),
        in_specs=in_specs,
        out_specs=out_specs,
        out_shape=out_shape,
        compiler_params=_params(("arbitrary", "arbitrary")),
        name="in_proj",
    )(h, w, *casts)
    return outs[0], outs[1:]


STATE_LAG = 2
MIXER_BLOCKS_PER_STEP = 256


def _mixer_kernel(*refs, tc, hb, n_chunks, has_cache, layer):
    if has_cache:
        (p_ref, convw_ref, lbl_ref, hgn_ref, cbuf_ref, s0_ref,
         y_ref, nconv_ref, nst_ref, st_ref, cu_ref) = refs
    else:
        p_ref, convw_ref, lbl_ref, hgn_ref, y_ref, nconv_ref, nst_ref, st_ref, cu_ref = refs
    SEG_B, SEG_C, SEG_U, SEG_Q, SEG_FZ, SEG_I, SEG_OG = range(7)
    c = pl.program_id(2)

    @pl.when(c == 0)
    def _init():
        for hh in range(hb):
            if has_cache:
                st_ref[hh] = s0_ref[0, hh].T
                cu_ref[hh, SUBLANES - (CONV_W - 1):SUBLANES, :] = cbuf_ref[0, :, hh * LANE:(hh + 1) * LANE]
            else:
                st_ref[hh] = jnp.zeros((HEAD_DIM, HEAD_DIM), F32)
                cu_ref[hh, 0:SUBLANES, :] = jnp.zeros((SUBLANES, LANE), F32)

    hgn = hgn_ref[...]
    sub = lax.broadcasted_iota(jnp.int32, (SUBLANES, LANE), 0)
    tri = (lax.broadcasted_iota(jnp.int32, (SCAN_BLOCK, SCAN_BLOCK), 0)
           >= lax.broadcasted_iota(jnp.int32, (SCAN_BLOCK, SCAN_BLOCK), 1))
    mid = SCAN_BLOCK // 2
    nt = (((1,), (1,)), ((), ()))
    tn = (((0,), (0,)), ((), ()))

    def head_consts(hh):
        lanes = slice(hh * LANE, (hh + 1) * LANE)
        logits = lbl_ref[:, lanes]
        ex = jnp.exp(logits - jnp.max(logits, axis=0, keepdims=True))
        lb = jnp.sum(ex[0:layer + 1, :], axis=0, keepdims=True) / jnp.sum(ex, axis=0, keepdims=True)
        return lb, 1.0 - lb, convw_ref[0:1, lanes], convw_ref[1:2, lanes], convw_ref[2:3, lanes]

    consts = [head_consts(hh) for hh in range(hb)]

    def prefix8(x):
        sh = 1
        while sh < SUBLANES:
            x = x + jnp.where(sub >= sh, pltpu.roll(x, sh, 0), 0.0)
            sh *= 2
        return x

    def conv_block(hh, lo):
        _, _, w0, w1, w2 = consts[hh]
        rows = pl.ds(lo, SCAN_BLOCK)
        cu = p_ref[hh, SEG_C, rows, :] * p_ref[hh, SEG_U, rows, :]
        cu_ref[hh, pl.ds(SUBLANES + lo, SCAN_BLOCK), :] = cu
        s1 = cu_ref[hh, pl.ds(SUBLANES - 1 + lo, SCAN_BLOCK), :]
        s2 = cu_ref[hh, pl.ds(SUBLANES - 2 + lo, SCAN_BLOCK), :]
        conv = w0 * s2 + w1 * s1 + w2 * cu
        y_ref[0, rows, hh * LANE:(hh + 1) * LANE] = (p_ref[hh, SEG_B, rows, :] * conv).astype(y_ref.dtype)

    def block_terms(hh, lo):
        lb, one_m_lb = consts[hh][0:2]
        rows = pl.ds(lo, SCAN_BLOCK)
        f = lb + one_m_lb * jax.nn.sigmoid(p_ref[hh, SEG_FZ, rows, :])
        g2 = jnp.log2(f)
        k = 1.0 - f
        q = _silu(p_ref[hh, SEG_Q, rows, :])
        v = p_ref[hh, SEG_I, rows, :].astype(BF16)
        parts = []
        total = None
        for p in range(SCAN_BLOCK // SUBLANES):
            part = prefix8(g2[p * SUBLANES:(p + 1) * SUBLANES, :])
            if total is not None:
                part = part + total
            total = part[SUBLANES - 1:SUBLANES, :]
            parts.append(part)
        lc = jnp.concatenate(parts, axis=0)
        ref = lc[mid - 1:mid, :]
        qd = (q * jnp.exp2(lc - ref)).astype(BF16)
        kd = (k * jnp.exp2(ref - lc)).astype(BF16)
        att = lax.dot_general(qd, kd, nt, preferred_element_type=F32)
        att = jnp.where(tri, att, 0.0).astype(BF16)
        return q, k, v, lc, total, att

    def state_free(hh, lo):
        qa, ka, va, lca, lla, att_a = block_terms(hh, lo)
        qb, kb, vb, lcb, llb, att_b = block_terms(hh, lo + SCAN_BLOCK)
        qe_a = (qa * jnp.exp2(lca)).astype(BF16)
        qe_b = (qb * jnp.exp2(lcb)).astype(BF16)
        kdec_a = (ka * jnp.exp2(lla - lca)).astype(BF16)
        kdec_b = (kb * jnp.exp2(llb - lcb)).astype(BF16)
        cross = lax.dot_general(qe_b, kdec_a, nt, preferred_element_type=F32).astype(BF16)
        qe = jnp.concatenate([qe_a, (qb * jnp.exp2(lcb + lla)).astype(BF16)], axis=0)
        kend = jnp.concatenate([(ka * jnp.exp2(lla + llb - lca)).astype(BF16), kdec_b], axis=0)
        upd = lax.dot_general(jnp.concatenate([va, vb], axis=0), kend, tn,
                              preferred_element_type=F32)
        return (att_a, att_b, cross, va, vb), upd, jnp.exp2(lla + llb), qe

    pair = 2 * SCAN_BLOCK
    pairs = [(hh, j * pair) for hh in range(hb) for j in range(tc // pair)]
    st = [st_ref[hh] for hh in range(hb)]
    pending = {}
    for n in range(len(pairs) + STATE_LAG):
        if n < len(pairs):
            hh, lo = pairs[n]
            conv_block(hh, lo)
            conv_block(hh, lo + SCAN_BLOCK)
            pending[n] = state_free(hh, lo)
        if n >= STATE_LAG:
            hh, lo = pairs[n - STATE_LAG]
            (att_a, att_b, cross, va, vb), upd, decay, qe = pending.pop(n - STATE_LAG)
            o_a = jnp.dot(att_a, va, preferred_element_type=F32)
            o_b = (jnp.dot(att_b, vb, preferred_element_type=F32)
                   + jnp.dot(cross, va, preferred_element_type=F32))
            o = (jnp.concatenate([o_a, o_b], axis=0)
                 + lax.dot_general(qe, st[hh].astype(BF16), nt, preferred_element_type=F32))
            st[hh] = st[hh] * decay + upd
            rows = pl.ds(lo, pair)
            ms = jnp.mean(o * o, axis=-1, keepdims=True)
            yb = o * lax.rsqrt(ms + NORM_EPS) * (hgn * _silu(p_ref[hh, SEG_OG, rows, :]))
            y_ref[1, rows, hh * LANE:(hh + 1) * LANE] = yb.astype(y_ref.dtype)
    for hh in range(hb):
        st_ref[hh] = st[hh]
        tail = cu_ref[hh, SUBLANES + tc - (CONV_W - 1):SUBLANES + tc, :]
        cu_ref[hh, SUBLANES - (CONV_W - 1):SUBLANES, :] = tail
        nconv_ref[0, :, hh * LANE:(hh + 1) * LANE] = tail

    @pl.when(c == n_chunks - 1)
    def _fin():
        for hh in range(hb):
            nst_ref[0, hh] = st[hh].T


def _mixer(proj, conv_w, lb_logits, hg_norm, *, row0, nseq, t, tc, hb, n_groups, layer, cache=None, state=None):
    has_cache = cache is not None
    if has_cache != (state is not None):
        raise ValueError("conv cache and recurrent state come together")
    n_chunks = t // tc
    rb0 = row0 // tc
    n_hblk = n_groups // hb

    n_seg = proj.shape[1]
    in_specs = [
        pl.BlockSpec((hb, n_seg, tc, LANE), lambda b, h, c: (h, 0, rb0 + b * n_chunks + c, 0)),
        pl.BlockSpec((CONV_W, hb * LANE), lambda b, h, c: (0, h)),
        pl.BlockSpec((lb_logits.shape[0], hb * LANE), lambda b, h, c: (0, h)),
        pl.BlockSpec((1, LANE), lambda b, h, c: (0, 0)),
    ]
    args = [proj, conv_w, lb_logits, hg_norm]
    if has_cache:
        in_specs += [
            pl.BlockSpec((1, CONV_W - 1, hb * LANE), lambda b, h, c: (b, 0, h)),
            pl.BlockSpec((1, hb, HEAD_DIM, HEAD_DIM), lambda b, h, c: (b, h, 0, 0)),
        ]
        args += [cache, state]
    d_half = n_groups * LANE
    out_specs = [
        pl.BlockSpec((2, tc, hb * LANE), lambda b, h, c: (0, b * n_chunks + c, h)),
        pl.BlockSpec((1, CONV_W - 1, hb * LANE), lambda b, h, c: (b, 0, h)),
        pl.BlockSpec((1, hb, HEAD_DIM, HEAD_DIM), lambda b, h, c: (b, h, 0, 0)),
    ]
    out_shape = [
        jax.ShapeDtypeStruct((2, nseq * t, d_half), BF16),
        jax.ShapeDtypeStruct((nseq, CONV_W - 1, d_half), F32),
        jax.ShapeDtypeStruct((nseq, n_groups, HEAD_DIM, HEAD_DIM), F32),
    ]
    kern = functools.partial(_mixer_kernel, tc=tc, hb=hb, n_chunks=n_chunks, has_cache=has_cache,
                             layer=layer)
    return pl.pallas_call(
        kern,
        grid=(nseq, n_hblk, n_chunks),
        in_specs=in_specs,
        out_specs=out_specs,
        out_shape=out_shape,
        scratch_shapes=[pltpu.VMEM((hb, HEAD_DIM, HEAD_DIM), F32),
                        pltpu.VMEM((hb, SUBLANES + tc, LANE), F32)],
        compiler_params=_params(("parallel", "parallel", "arbitrary")),
        name="mixer_cached" if has_cache else "mixer_prompt",
    )(*args)


def _out_proj_kernel(yp_ref, ys_ref, w_ref, xp_ref, xs_ref, wn_ref, o_ref, xw_ref, ssq_ref, *, n_p):
    i = pl.program_id(1)

    def emit(y_ref, x_ref):
        y = jnp.concatenate([y_ref[0], y_ref[1]], axis=1)
        x1 = x_ref[...] + jnp.dot(y, w_ref[...], preferred_element_type=F32)
        o_ref[...] = x1
        xw_ref[...] = (x1 * wn_ref[...]).astype(xw_ref.dtype)
        sq = x1 * x1
        acc = sq[:, 0:LANE]
        for g in range(1, sq.shape[1] // LANE):
            acc = acc + sq[:, g * LANE:(g + 1) * LANE]
        ssq_ref[...] = acc

    @pl.when(i < n_p)
    def _():
        emit(yp_ref, xp_ref)

    @pl.when(i >= n_p)
    def _():
        emit(ys_ref, xs_ref)


def _out_proj(y_p, y_s, w, xp, xs, w_norm, tm, tn):
    _, mp, dh = y_p.shape
    ms = y_s.shape[1]
    kdim, n = w.shape
    n_p, n_s = mp // tm, ms // tm
    m = mp + ms

    def prow(j, i):
        return jnp.minimum(i, n_p - 1)

    def srow(j, i):
        return jnp.maximum(i - n_p, 0)

    return pl.pallas_call(
        functools.partial(_out_proj_kernel, n_p=n_p),
        grid=(n // tn, n_p + n_s),
        in_specs=[
            pl.BlockSpec((2, tm, dh), lambda j, i: (0, prow(j, i), 0)),
            pl.BlockSpec((2, tm, dh), lambda j, i: (0, srow(j, i), 0)),
            pl.BlockSpec((kdim, tn), lambda j, i: (0, j)),
            pl.BlockSpec((tm, tn), lambda j, i: (prow(j, i), j)),
            pl.BlockSpec((tm, tn), lambda j, i: (srow(j, i), j)),
            pl.BlockSpec((1, tn), lambda j, i: (0, j)),
        ],
        out_specs=[
            pl.BlockSpec((tm, tn), lambda j, i: (i, j)),
            pl.BlockSpec((tm, tn), lambda j, i: (i, j)),
            pl.BlockSpec((tm, LANE), lambda j, i: (i, j)),
        ],
        out_shape=[
            jax.ShapeDtypeStruct((m, n), F32),
            jax.ShapeDtypeStruct((m, n), BF16),
            jax.ShapeDtypeStruct((m, (n // tn) * LANE), F32),
        ],
        compiler_params=_params(("parallel", "parallel")),
        name="out_proj",
    )(y_p, y_s, w, xp, xs, w_norm)


FFN_UP_SUB_ROWS = 256


def _ffn_up_kernel(a_ref, ssq_ref, wg_ref, wu_ref, o_ref, *, sub_rows):
    wg = wg_ref[...].astype(BF16)
    wu = wu_ref[...].astype(BF16)
    for r0 in range(0, a_ref.shape[0], sub_rows):
        rows = pl.ds(r0, sub_rows)
        a = a_ref[rows, :]
        r = lax.rsqrt(jnp.sum(ssq_ref[rows, :], axis=-1, keepdims=True) / a.shape[1] + NORM_EPS)
        g = r * jnp.dot(a, wg, preferred_element_type=F32)
        u = r * jnp.dot(a, wu, preferred_element_type=F32)
        o_ref[rows, :] = (_silu(g) * u).astype(o_ref.dtype)


def _ffn_up(xw, ssq, wg, wu, tm, tn):
    m, k = xw.shape
    n = wg.shape[1]
    wspec = pl.BlockSpec((k, tn), lambda j, i: (0, j))
    return pl.pallas_call(
        functools.partial(_ffn_up_kernel, sub_rows=_largest_tile(tm, FFN_UP_SUB_ROWS, BF16_ROWS)),
        grid=(n // tn, m // tm),
        in_specs=[
            pl.BlockSpec((tm, k), lambda j, i: (i, 0)),
            pl.BlockSpec((tm, ssq.shape[1]), lambda j, i: (i, 0)),
            wspec, wspec,
        ],
        out_specs=pl.BlockSpec((tm, tn), lambda j, i: (i, j)),
        out_shape=jax.ShapeDtypeStruct((m, n), BF16),
        compiler_params=_params(("parallel", "parallel")),
        name="ffn_up",
    )(xw, ssq, wg, wu)


def _ffn_down_kernel(a_ref, w_ref, x_ref, o_ref):
    o_ref[...] = x_ref[...] + jnp.dot(a_ref[...], w_ref[...], preferred_element_type=F32)


def _ffn_down(act, w, x1, tm, tn):
    m, kf = act.shape
    n = w.shape[1]
    return pl.pallas_call(
        _ffn_down_kernel,
        grid=(n // tn, m // tm),
        in_specs=[
            pl.BlockSpec((tm, kf), lambda j, i: (i, 0)),
            pl.BlockSpec((kf, tn), lambda j, i: (0, j)),
            pl.BlockSpec((tm, tn), lambda j, i: (i, j)),
        ],
        out_specs=pl.BlockSpec((tm, tn), lambda j, i: (i, j)),
        out_shape=jax.ShapeDtypeStruct((m, n), F32),
        compiler_params=_params(("parallel", "parallel")),
        name="ffn_down",
    )(act, w, x1)


def _mixer_tiles(t, n_groups):
    tc = _largest_tile(t, MIXER_BLOCKS_PER_STEP * SCAN_BLOCK, 2 * SCAN_BLOCK)
    hb = 1
    while hb < n_groups and 2 * hb * tc <= MIXER_BLOCKS_PER_STEP * SCAN_BLOCK and n_groups % (2 * hb) == 0:
        hb *= 2
    return dict(tc=tc, hb=hb)


def _tiles(mp, ms, tp, ts, n_groups, d, n_in, d_ff):
    m = mp + ms
    tr = _largest_tile(_gcd(mp, ms), 512, BF16_ROWS)
    return dict(
        tr=tr,
        tm=_largest_tile(m, 1536, BF16_ROWS),
        tn_in=_largest_tile(n_in, 512, LANE),
        tn_out=_largest_tile(d, 1024, LANE),
        tn_up=_largest_tile(d_ff, 256, LANE),
        tm_down=_largest_tile(m, 528, BF16_ROWS),
        tn_down=_largest_tile(d, 512, LANE),
        mix_p=_mixer_tiles(tp, n_groups),
        mix_s=_mixer_tiles(ts, n_groups),
    )


def kernel(x_prompt, x_sample, cache_conv, state_hgrn, norm_mix, w_in, conv_w, lb_logits, hg_norm,
           w_out, norm_ffn, w_gate, w_up, w_down, norm_final):
    bp, tp, d = x_prompt.shape
    bs, ts, _ = x_sample.shape
    depth = w_in.shape[0]
    if depth != 1:
        raise NotImplementedError("one layer")
    layer = 0
    n_in = w_in.shape[2]
    d_ff = w_gate.shape[2]
    d_half = conv_w.shape[2]
    n_groups = d_half // LANE
    if n_in != 7 * d_half or state_hgrn.shape[2] != n_groups or w_out.shape[1] != 2 * d_half:
        raise ValueError("unexpected layer geometry")
    mp, ms = bp * tp, bs * ts
    m = mp + ms
    t = _tiles(mp, ms, tp, ts, n_groups, d, n_in, d_ff)

    xp = x_prompt.reshape(mp, d)
    xs = x_sample.reshape(ms, d)
    h = _norm_two_src(xp, xs, norm_mix[layer].reshape(1, d), t["tr"])
    proj, (w_out_b, w_down_b) = _in_proj(h, w_in[layer], t["tm"], t["tn_in"], n_groups,
                                         casts=[w_out[layer], w_down[layer]])

    mix_args = dict(n_groups=n_groups, layer=layer)
    hgn = hg_norm[layer].reshape(1, HEAD_DIM)
    y_p, conv_p, hg_p = _mixer(proj, conv_w[layer], lb_logits, hgn, row0=0, nseq=bp, t=tp,
                                      **t["mix_p"], **mix_args)
    y_s, conv_s, hg_s = _mixer(proj, conv_w[layer], lb_logits, hgn, row0=mp, nseq=bs, t=ts,
                                      cache=cache_conv[layer], state=state_hgrn[layer],
                                      **t["mix_s"], **mix_args)

    x1, x1w, ssq = _out_proj(y_p, y_s, w_out_b, xp, xs, norm_ffn[layer].reshape(1, d),
                             t["tr"], t["tn_out"])
    act = _ffn_up(x1w, ssq, w_gate[layer], w_up[layer], t["tm"], t["tn_up"])
    x2 = _ffn_down(act, w_down_b, x1, t["tm_down"], t["tn_down"])
    y_p, y_s = _norm_two_sink(x2, norm_final.reshape(1, d), mp, t["tr"])

    return (y_p.reshape(bp, tp, d), y_s.reshape(bs, ts, d),
            conv_p[None], hg_p[None].astype(x_prompt.dtype),
            conv_s[None].astype(cache_conv.dtype), hg_s[None].astype(state_hgrn.dtype))
```

```python
import functools

import jax
import jax.numpy as jnp
from jax import lax
from jax.experimental import pallas as pl
from jax.experimental.pallas import tpu as pltpu

LANE = 128
SUBLANES = 8
BF16_ROWS = 16
SCAN_BLOCK = 16
HEAD_DIM = 128
CONV_W = 3
NORM_EPS = 1e-6
V7X_VMEM_BYTES = 64 * 1024 * 1024
VMEM_LIMIT = V7X_VMEM_BYTES - 3 * 1024 * 1024

F32 = jnp.float32
BF16 = jnp.bfloat16


def _largest_tile(n, cap, quantum):
    best = None
    for t in range(quantum, min(n, cap) + 1, quantum):
        if n % t == 0:
            best = t
    if best is None:
        raise ValueError(f"no tile for n={n} cap={cap} quantum={quantum}")
    return best


def _gcd(a, b):
    while b:
        a, b = b, a % b
    return a


def _params(sem):
    return pltpu.CompilerParams(dimension_semantics=sem, vmem_limit_bytes=VMEM_LIMIT)


def _rms_rows(x, w):
    ms = jnp.mean(x * x, axis=-1, keepdims=True)
    return x * lax.rsqrt(ms + NORM_EPS) * w


def _silu(x):
    return x * jax.nn.sigmoid(x)


def _norm_two_src_kernel(xp_ref, xs_ref, w_ref, o_ref, *, n_p):
    i = pl.program_id(0)

    @pl.when(i < n_p)
    def _():
        o_ref[...] = _rms_rows(xp_ref[...], w_ref[...]).astype(o_ref.dtype)

    @pl.when(i >= n_p)
    def _():
        o_ref[...] = _rms_rows(xs_ref[...], w_ref[...]).astype(o_ref.dtype)


def _norm_two_src(xp, xs, w, tr):
    mp, d = xp.shape
    ms = xs.shape[0]
    n_p, n_s = mp // tr, ms // tr
    return pl.pallas_call(
        functools.partial(_norm_two_src_kernel, n_p=n_p),
        grid=(n_p + n_s,),
        in_specs=[
            pl.BlockSpec((tr, d), lambda i: (jnp.minimum(i, n_p - 1), 0)),
            pl.BlockSpec((tr, d), lambda i: (jnp.maximum(i - n_p, 0), 0)),
            pl.BlockSpec((1, d), lambda i: (0, 0)),
        ],
        out_specs=pl.BlockSpec((tr, d), lambda i: (i, 0)),
        out_shape=jax.ShapeDtypeStruct((mp + ms, d), BF16),
        compiler_params=_params(("parallel",)),
        name="norm1",
    )(xp, xs, w)


def _norm_two_sink_kernel(x_ref, w_ref, op_ref, os_ref, *, n_p):
    i = pl.program_id(0)

    @pl.when(i < n_p)
    def _():
        op_ref[...] = _rms_rows(x_ref[...], w_ref[...])

    @pl.when(i >= n_p)
    def _():
        os_ref[...] = _rms_rows(x_ref[...], w_ref[...])


def _norm_two_sink(x, w, mp, tr):
    m, d = x.shape
    ms = m - mp
    n_p, n_s = mp // tr, ms // tr
    return pl.pallas_call(
        functools.partial(_norm_two_sink_kernel, n_p=n_p),
        grid=(n_p + n_s,),
        in_specs=[pl.BlockSpec((tr, d), lambda i: (i, 0)), pl.BlockSpec((1, d), lambda i: (0, 0))],
        out_specs=[
            pl.BlockSpec((tr, d), lambda i: (jnp.minimum(i, n_p - 1), 0)),
            pl.BlockSpec((tr, d), lambda i: (jnp.maximum(i - n_p, 0), 0)),
        ],
        out_shape=[jax.ShapeDtypeStruct((mp, d), F32), jax.ShapeDtypeStruct((ms, d), F32)],
        compiler_params=_params(("arbitrary",)),
        name="final_norm",
    )(x, w)


def _cast_rows(r, n_steps):
    for rows in range(BF16_ROWS, r + 1, BF16_ROWS):
        if r % rows == 0 and r // rows <= n_steps:
            return rows
    raise ValueError(f"cannot cast {r} rows in {n_steps} grid steps")


def _in_proj_kernel(*refs, n_cast):
    a_ref, b_ref = refs[0:2]
    cast_in = refs[2:2 + n_cast]
    o_ref = refs[2 + n_cast]
    cast_out = refs[3 + n_cast:3 + 2 * n_cast]
    for src, dst in zip(cast_in, cast_out):
        dst[...] = src[...].astype(dst.dtype)

    acc = jnp.dot(a_ref[...], b_ref[...].astype(BF16), preferred_element_type=F32)
    for g in range(o_ref.shape[0]):
        o_ref[g, 0] = acc[:, g * LANE:(g + 1) * LANE]


def _in_proj(h, w, tm, tn, n_groups, casts):
    m, k = h.shape
    n = w.shape[1]
    ni, nj = m // tm, n // tn
    gpt = tn // LANE
    tps = n_groups // gpt
    in_specs = [pl.BlockSpec((tm, k), lambda i, j: (i, 0)), pl.BlockSpec((k, tn), lambda i, j: (0, j))]
    out_specs = [pl.BlockSpec((gpt, 1, tm, LANE), lambda i, j: (j % tps, j // tps, i, 0))]
    out_shape = [jax.ShapeDtypeStruct((n_groups, n // (n_groups * LANE), m, LANE), F32)]
    for arr in casts:
        r, c = arr.shape
        rows = _cast_rows(r, ni * nj)
        spec = pl.BlockSpec((rows, c), lambda i, j, last=r // rows - 1: (jnp.minimum(i * nj + j, last), 0))
        in_specs.append(spec)
        out_specs.append(spec)
        out_shape.append(jax.ShapeDtypeStruct((r, c), BF16))
    outs = pl.pallas_call(
        functools.partial(_in_proj_kernel, n_cast=len(casts)),
        grid=(ni, nj),
        in_specs=in_specs,
        out_specs=out_specs,
        out_shape=out_shape,
        compiler_params=_params(("arbitrary", "arbitrary")),
        name="in_proj",
    )(h, w, *casts)
    return outs[0], outs[1:]


STATE_LAG = 2
MIXER_BLOCKS_PER_STEP = 256


def _mixer_kernel(*refs, tc, hb, n_chunks, has_cache, layer):
    if has_cache:
        (p_ref, convw_ref, lbl_ref, hgn_ref, cbuf_ref, s0_ref,
         y_ref, nconv_ref, nst_ref, st_ref, cu_ref) = refs
    else:
        p_ref, convw_ref, lbl_ref, hgn_ref, y_ref, nconv_ref, nst_ref, st_ref, cu_ref = refs
    SEG_B, SEG_C, SEG_U, SEG_Q, SEG_FZ, SEG_I, SEG_OG = range(7)
    c = pl.program_id(2)

    @pl.when(c == 0)
    def _init():
        for hh in range(hb):
            if has_cache:
                st_ref[hh] = s0_ref[0, hh].T
                cu_ref[hh, SUBLANES - (CONV_W - 1):SUBLANES, :] = cbuf_ref[0, :, hh * LANE:(hh + 1) * LANE]
            else:
                st_ref[hh] = jnp.zeros((HEAD_DIM, HEAD_DIM), F32)
                cu_ref[hh, 0:SUBLANES, :] = jnp.zeros((SUBLANES, LANE), F32)

    hgn = hgn_ref[...]
    sub = lax.broadcasted_iota(jnp.int32, (SUBLANES, LANE), 0)
    tri = (lax.broadcasted_iota(jnp.int32, (SCAN_BLOCK, SCAN_BLOCK), 0)
           >= lax.broadcasted_iota(jnp.int32, (SCAN_BLOCK, SCAN_BLOCK), 1))
    mid = SCAN_BLOCK // 2
    nt = (((1,), (1,)), ((), ()))
    tn = (((0,), (0,)), ((), ()))

    def head_consts(hh):
        lanes = slice(hh * LANE, (hh + 1) * LANE)
        logits = lbl_ref[:, lanes]
        ex = jnp.exp(logits - jnp.max(logits, axis=0, keepdims=True))
        lb = jnp.sum(ex[0:layer + 1, :], axis=0, keepdims=True) / jnp.sum(ex, axis=0, keepdims=True)
        return lb, 1.0 - lb, convw_ref[0:1, lanes], convw_ref[1:2, lanes], convw_ref[2:3, lanes]

    consts = [head_consts(hh) for hh in range(hb)]

    def prefix8(x):
        sh = 1
        while sh < SUBLANES:
            x = x + jnp.where(sub >= sh, pltpu.roll(x, sh, 0), 0.0)
            sh *= 2
        return x

    def conv_block(hh, lo):
        _, _, w0, w1, w2 = consts[hh]
        rows = pl.ds(lo, SCAN_BLOCK)
        cu = p_ref[hh, SEG_C, rows, :] * p_ref[hh, SEG_U, rows, :]
        cu_ref[hh, pl.ds(SUBLANES + lo, SCAN_BLOCK), :] = cu
        s1 = cu_ref[hh, pl.ds(SUBLANES - 1 + lo, SCAN_BLOCK), :]
        s2 = cu_ref[hh, pl.ds(SUBLANES - 2 + lo, SCAN_BLOCK), :]
        conv = w0 * s2 + w1 * s1 + w2 * cu
        y_ref[0, rows, hh * LANE:(hh + 1) * LANE] = (p_ref[hh, SEG_B, rows, :] * conv).astype(y_ref.dtype)

    def block_terms(hh, lo):
        lb, one_m_lb = consts[hh][0:2]
        rows = pl.ds(lo, SCAN_BLOCK)
        f = lb + one_m_lb * jax.nn.sigmoid(p_ref[hh, SEG_FZ, rows, :])
        g2 = jnp.log2(f)
        k = 1.0 - f
        q = _silu(p_ref[hh, SEG_Q, rows, :])
        v = p_ref[hh, SEG_I, rows, :].astype(BF16)
        parts = []
        total = None
        for p in range(SCAN_BLOCK // SUBLANES):
            part = prefix8(g2[p * SUBLANES:(p + 1) * SUBLANES, :])
            if total is not None:
                part = part + total
            total = part[SUBLANES - 1:SUBLANES, :]
            parts.append(part)
        lc = jnp.concatenate(parts, axis=0)
        ref = lc[mid - 1:mid, :]
        qd = (q * jnp.exp2(lc - ref)).astype(BF16)
        kd = (k * jnp.exp2(ref - lc)).astype(BF16)
        att = lax.dot_general(qd, kd, nt, preferred_element_type=F32)
        att = jnp.where(tri, att, 0.0).astype(BF16)
        return q, k, v, lc, total, att

    def state_free(hh, lo):
        qa, ka, va, lca, lla, att_a = block_terms(hh, lo)
        qb, kb, vb, lcb, llb, att_b = block_terms(hh, lo + SCAN_BLOCK)
        qe_a = (qa * jnp.exp2(lca)).astype(BF16)
        qe_b = (qb * jnp.exp2(lcb)).astype(BF16)
        kdec_a = (ka * jnp.exp2(lla - lca)).astype(BF16)
        kdec_b = (kb * jnp.exp2(llb - lcb)).astype(BF16)
        cross = lax.dot_general(qe_b, kdec_a, nt, preferred_element_type=F32).astype(BF16)
        qe = jnp.concatenate([qe_a, (qb * jnp.exp2(lcb + lla)).astype(BF16)], axis=0)
        kend = jnp.concatenate([(ka * jnp.exp2(lla + llb - lca)).astype(BF16), kdec_b], axis=0)
        upd = lax.dot_general(jnp.concatenate([va, vb], axis=0), kend, tn,
                              preferred_element_type=F32)
        return (att_a, att_b, cross, va, vb), upd, jnp.exp2(lla + llb), qe

    pair = 2 * SCAN_BLOCK
    pairs = [(hh, j * pair) for hh in range(hb) for j in range(tc // pair)]
    st = [st_ref[hh] for hh in range(hb)]
    pending = {}
    for n in range(len(pairs) + STATE_LAG):
        if n < len(pairs):
            hh, lo = pairs[n]
            conv_block(hh, lo)
            conv_block(hh, lo + SCAN_BLOCK)
            pending[n] = state_free(hh, lo)
        if n >= STATE_LAG:
            hh, lo = pairs[n - STATE_LAG]
            (att_a, att_b, cross, va, vb), upd, decay, qe = pending.pop(n - STATE_LAG)
            o_a = jnp.dot(att_a, va, preferred_element_type=F32)
            o_b = (jnp.dot(att_b, vb, preferred_element_type=F32)
                   + jnp.dot(cross, va, preferred_element_type=F32))
            o = (jnp.concatenate([o_a, o_b], axis=0)
                 + lax.dot_general(qe, st[hh].astype(BF16), nt, preferred_element_type=F32))
            st[hh] = st[hh] * decay + upd
            rows = pl.ds(lo, pair)
            ms = jnp.mean(o * o, axis=-1, keepdims=True)
            yb = o * lax.rsqrt(ms + NORM_EPS) * (hgn * _silu(p_ref[hh, SEG_OG, rows, :]))
            y_ref[1, rows, hh * LANE:(hh + 1) * LANE] = yb.astype(y_ref.dtype)
    for hh in range(hb):
        st_ref[hh] = st[hh]
        tail = cu_ref[hh, SUBLANES + tc - (CONV_W - 1):SUBLANES + tc, :]
        cu_ref[hh, SUBLANES - (CONV_W - 1):SUBLANES, :] = tail
        nconv_ref[0, :, hh * LANE:(hh + 1) * LANE] = tail

    @pl.when(c == n_chunks - 1)
    def _fin():
        for hh in range(hb):
            nst_ref[0, hh] = st[hh].T


def _mixer(proj, conv_w, lb_logits, hg_norm, *, row0, nseq, t, tc, hb, n_groups, layer, cache=None, state=None):
    has_cache = cache is not None
    if has_cache != (state is not None):
        raise ValueError("conv cache and recurrent state come together")
    n_chunks = t // tc
    rb0 = row0 // tc
    n_hblk = n_groups // hb

    n_seg = proj.shape[1]
    in_specs = [
        pl.BlockSpec((hb, n_seg, tc, LANE), lambda b, h, c: (h, 0, rb0 + b * n_chunks + c, 0)),
        pl.BlockSpec((CONV_W, hb * LANE), lambda b, h, c: (0, h)),
        pl.BlockSpec((lb_logits.shape[0], hb * LANE), lambda b, h, c: (0, h)),
        pl.BlockSpec((1, LANE), lambda b, h, c: (0, 0)),
    ]
    args = [proj, conv_w, lb_logits, hg_norm]
    if has_cache:
        in_specs += [
            pl.BlockSpec((1, CONV_W - 1, hb * LANE), lambda b, h, c: (b, 0, h)),
            pl.BlockSpec((1, hb, HEAD_DIM, HEAD_DIM), lambda b, h, c: (b, h, 0, 0)),
        ]
        args += [cache, state]
    d_half = n_groups * LANE
    out_specs = [
        pl.BlockSpec((2, tc, hb * LANE), lambda b, h, c: (0, b * n_chunks + c, h)),
        pl.BlockSpec((1, CONV_W - 1, hb * LANE), lambda b, h, c: (b, 0, h)),
        pl.BlockSpec((1, hb, HEAD_DIM, HEAD_DIM), lambda b, h, c: (b, h, 0, 0)),
    ]
    out_shape = [
        jax.ShapeDtypeStruct((2, nseq * t, d_half), BF16),
        jax.ShapeDtypeStruct((nseq, CONV_W - 1, d_half), F32),
        jax.ShapeDtypeStruct((nseq, n_groups, HEAD_DIM, HEAD_DIM), F32),
    ]
    kern = functools.partial(_mixer_kernel, tc=tc, hb=hb, n_chunks=n_chunks, has_cache=has_cache,
                             layer=layer)
    return pl.pallas_call(
        kern,
        grid=(nseq, n_hblk, n_chunks),
        in_specs=in_specs,
        out_specs=out_specs,
        out_shape=out_shape,
        scratch_shapes=[pltpu.VMEM((hb, HEAD_DIM, HEAD_DIM), F32),
                        pltpu.VMEM((hb, SUBLANES + tc, LANE), F32)],
        compiler_params=_params(("parallel", "parallel", "arbitrary")),
        name="mixer_cached" if has_cache else "mixer_prompt",
    )(*args)


def _out_proj_kernel(yp_ref, ys_ref, w_ref, xp_ref, xs_ref, wn_ref, o_ref, xw_ref, ssq_ref, *, n_p):
    i = pl.program_id(1)

    def emit(y_ref, x_ref):
        y = jnp.concatenate([y_ref[0], y_ref[1]], axis=1)
        x1 = x_ref[...] + jnp.dot(y, w_ref[...], preferred_element_type=F32)
        o_ref[...] = x1
        xw_ref[...] = (x1 * wn_ref[...]).astype(xw_ref.dtype)
        sq = x1 * x1
        acc = sq[:, 0:LANE]
        for g in range(1, sq.shape[1] // LANE):
            acc = acc + sq[:, g * LANE:(g + 1) * LANE]
        ssq_ref[...] = acc

    @pl.when(i < n_p)
    def _():
        emit(yp_ref, xp_ref)

    @pl.when(i >= n_p)
    def _():
        emit(ys_ref, xs_ref)


def _out_proj(y_p, y_s, w, xp, xs, w_norm, tm, tn):
    _, mp, dh = y_p.shape
    ms = y_s.shape[1]
    kdim, n = w.shape
    n_p, n_s = mp // tm, ms // tm
    m = mp + ms

    def prow(j, i):
        return jnp.minimum(i, n_p - 1)

    def srow(j, i):
        return jnp.maximum(i - n_p, 0)

    return pl.pallas_call(
        functools.partial(_out_proj_kernel, n_p=n_p),
        grid=(n // tn, n_p + n_s),
        in_specs=[
            pl.BlockSpec((2, tm, dh), lambda j, i: (0, prow(j, i), 0)),
            pl.BlockSpec((2, tm, dh), lambda j, i: (0, srow(j, i), 0)),
            pl.BlockSpec((kdim, tn), lambda j, i: (0, j)),
            pl.BlockSpec((tm, tn), lambda j, i: (prow(j, i), j)),
            pl.BlockSpec((tm, tn), lambda j, i: (srow(j, i), j)),
            pl.BlockSpec((1, tn), lambda j, i: (0, j)),
        ],
        out_specs=[
            pl.BlockSpec((tm, tn), lambda j, i: (i, j)),
            pl.BlockSpec((tm, tn), lambda j, i: (i, j)),
            pl.BlockSpec((tm, LANE), lambda j, i: (i, j)),
        ],
        out_shape=[
            jax.ShapeDtypeStruct((m, n), F32),
            jax.ShapeDtypeStruct((m, n), BF16),
            jax.ShapeDtypeStruct((m, (n // tn) * LANE), F32),
        ],
        compiler_params=_params(("parallel", "parallel")),
        name="out_proj",
    )(y_p, y_s, w, xp, xs, w_norm)


FFN_UP_SUB_ROWS = 256


def _ffn_up_kernel(a_ref, ssq_ref, wg_ref, wu_ref, o_ref, *, sub_rows):
    wg = wg_ref[...].astype(BF16)
    wu = wu_ref[...].astype(BF16)
    for r0 in range(0, a_ref.shape[0], sub_rows):
        rows = pl.ds(r0, sub_rows)
        a = a_ref[rows, :]
        r = lax.rsqrt(jnp.sum(ssq_ref[rows, :], axis=-1, keepdims=True) / a.shape[1] + NORM_EPS)
        g = r * jnp.dot(a, wg, preferred_element_type=F32)
        u = r * jnp.dot(a, wu, preferred_element_type=F32)
        o_ref[rows, :] = (_silu(g) * u).astype(o_ref.dtype)


def _ffn_up(xw, ssq, wg, wu, tm, tn):
    m, k = xw.shape
    n = wg.shape[1]
    wspec = pl.BlockSpec((k, tn), lambda j, i: (0, j))
    return pl.pallas_call(
        functools.partial(_ffn_up_kernel, sub_rows=_largest_tile(tm, FFN_UP_SUB_ROWS, BF16_ROWS)),
        grid=(n // tn, m // tm),
        in_specs=[
            pl.BlockSpec((tm, k), lambda j, i: (i, 0)),
            pl.BlockSpec((tm, ssq.shape[1]), lambda j, i: (i, 0)),
            wspec, wspec,
        ],
        out_specs=pl.BlockSpec((tm, tn), lambda j, i: (i, j)),
        out_shape=jax.ShapeDtypeStruct((m, n), BF16),
        compiler_params=_params(("parallel", "parallel")),
        name="ffn_up",
    )(xw, ssq, wg, wu)


def _ffn_down_kernel(a_ref, w_ref, x_ref, o_ref):
    o_ref[...] = x_ref[...] + jnp.dot(a_ref[...], w_ref[...], preferred_element_type=F32)


def _ffn_down(act, w, x1, tm, tn):
    m, kf = act.shape
    n = w.shape[1]
    return pl.pallas_call(
        _ffn_down_kernel,
        grid=(n // tn, m // tm),
        in_specs=[
            pl.BlockSpec((tm, kf), lambda j, i: (i, 0)),
            pl.BlockSpec((kf, tn), lambda j, i: (0, j)),
            pl.BlockSpec((tm, tn), lambda j, i: (i, j)),
        ],
        out_specs=pl.BlockSpec((tm, tn), lambda j, i: (i, j)),
        out_shape=jax.ShapeDtypeStruct((m, n), F32),
        compiler_params=_params(("parallel", "parallel")),
        name="ffn_down",
    )(act, w, x1)


def _mixer_tiles(t, n_groups):
    tc = _largest_tile(t, MIXER_BLOCKS_PER_STEP * SCAN_BLOCK, 2 * SCAN_BLOCK)
    hb = 1
    while hb < n_groups and 2 * hb * tc <= MIXER_BLOCKS_PER_STEP * SCAN_BLOCK and n_groups % (2 * hb) == 0:
        hb *= 2
    return dict(tc=tc, hb=hb)


def _tiles(mp, ms, tp, ts, n_groups, d, n_in, d_ff):
    m = mp + ms
    tr = _largest_tile(_gcd(mp, ms), 512, BF16_ROWS)
    return dict(
        tr=tr,
        tm=_largest_tile(m, 1536, BF16_ROWS),
        tn_in=_largest_tile(n_in, 512, LANE),
        tn_out=_largest_tile(d, 1024, LANE),
        tn_up=_largest_tile(d_ff, 256, LANE),
        tm_down=_largest_tile(m, 704, BF16_ROWS),
        tn_down=_largest_tile(d, 512, LANE),
        mix_p=_mixer_tiles(tp, n_groups),
        mix_s=_mixer_tiles(ts, n_groups),
    )


def kernel(x_prompt, x_sample, cache_conv, state_hgrn, norm_mix, w_in, conv_w, lb_logits, hg_norm,
           w_out, norm_ffn, w_gate, w_up, w_down, norm_final):
    bp, tp, d = x_prompt.shape
    bs, ts, _ = x_sample.shape
    depth = w_in.shape[0]
    if depth != 1:
        raise NotImplementedError("one layer")
    layer = 0
    n_in = w_in.shape[2]
    d_ff = w_gate.shape[2]
    d_half = conv_w.shape[2]
    n_groups = d_half // LANE
    if n_in != 7 * d_half or state_hgrn.shape[2] != n_groups or w_out.shape[1] != 2 * d_half:
        raise ValueError("unexpected layer geometry")
    mp, ms = bp * tp, bs * ts
    m = mp + ms
    t = _tiles(mp, ms, tp, ts, n_groups, d, n_in, d_ff)

    xp = x_prompt.reshape(mp, d)
    xs = x_sample.reshape(ms, d)
    h = _norm_two_src(xp, xs, norm_mix[layer].reshape(1, d), t["tr"])
    proj, (w_out_b, w_down_b) = _in_proj(h, w_in[layer], t["tm"], t["tn_in"], n_groups,
                                         casts=[w_out[layer], w_down[layer]])

    mix_args = dict(n_groups=n_groups, layer=layer)
    hgn = hg_norm[layer].reshape(1, HEAD_DIM)
    y_p, conv_p, hg_p = _mixer(proj, conv_w[layer], lb_logits, hgn, row0=0, nseq=bp, t=tp,
                                      **t["mix_p"], **mix_args)
    y_s, conv_s, hg_s = _mixer(proj, conv_w[layer], lb_logits, hgn, row0=mp, nseq=bs, t=ts,
                                      cache=cache_conv[layer], state=state_hgrn[layer],
                                      **t["mix_s"], **mix_args)

    x1, x1w, ssq = _out_proj(y_p, y_s, w_out_b, xp, xs, norm_ffn[layer].reshape(1, d),
                             t["tr"], t["tn_out"])
    act = _ffn_up(x1w, ssq, w_gate[layer], w_up[layer], t["tm"], t["tn_up"])
    x2 = _ffn_down(act, w_down_b, x1, t["tm_down"], t["tn_down"])
    y_p, y_s = _norm_two_sink(x2, norm_final.reshape(1, d), mp, t["tr"])

    return (y_p.reshape(bp, tp, d), y_s.reshape(bs, ts, d),
            conv_p[None], hg_p[None].astype(x_prompt.dtype),
            conv_s[None].astype(cache_conv.dtype), hg_s[None].astype(state_hgrn.dtype))
```

```python
import functools

import jax
import jax.numpy as jnp
from jax import lax
from jax.experimental import pallas as pl
from jax.experimental.pallas import tpu as pltpu

LANE = 128
SUBLANES = 8
BF16_ROWS = 16
SCAN_BLOCK = 16
HEAD_DIM = 128
CONV_W = 3
NORM_EPS = 1e-6
V7X_VMEM_BYTES = 64 * 1024 * 1024
VMEM_LIMIT = V7X_VMEM_BYTES - 3 * 1024 * 1024

F32 = jnp.float32
BF16 = jnp.bfloat16


def _largest_tile(n, cap, quantum):
    best = None
    for t in range(quantum, min(n, cap) + 1, quantum):
        if n % t == 0:
            best = t
    if best is None:
        raise ValueError(f"no tile for n={n} cap={cap} quantum={quantum}")
    return best


def _gcd(a, b):
    while b:
        a, b = b, a % b
    return a


def _params(sem):
    return pltpu.CompilerParams(dimension_semantics=sem, vmem_limit_bytes=VMEM_LIMIT)


def _rms_rows(x, w):
    ms = jnp.mean(x * x, axis=-1, keepdims=True)
    return x * lax.rsqrt(ms + NORM_EPS) * w


def _silu(x):
    return x * jax.nn.sigmoid(x)


def _norm_two_src_kernel(xp_ref, xs_ref, w_ref, o_ref, *, n_p):
    i = pl.program_id(0)

    @pl.when(i < n_p)
    def _():
        o_ref[...] = _rms_rows(xp_ref[...], w_ref[...]).astype(o_ref.dtype)

    @pl.when(i >= n_p)
    def _():
        o_ref[...] = _rms_rows(xs_ref[...], w_ref[...]).astype(o_ref.dtype)


def _norm_two_src(xp, xs, w, tr):
    mp, d = xp.shape
    ms = xs.shape[0]
    n_p, n_s = mp // tr, ms // tr
    return pl.pallas_call(
        functools.partial(_norm_two_src_kernel, n_p=n_p),
        grid=(n_p + n_s,),
        in_specs=[
            pl.BlockSpec((tr, d), lambda i: (jnp.minimum(i, n_p - 1), 0)),
            pl.BlockSpec((tr, d), lambda i: (jnp.maximum(i - n_p, 0), 0)),
            pl.BlockSpec((1, d), lambda i: (0, 0)),
        ],
        out_specs=pl.BlockSpec((tr, d), lambda i: (i, 0)),
        out_shape=jax.ShapeDtypeStruct((mp + ms, d), BF16),
        compiler_params=_params(("parallel",)),
        name="norm1",
    )(xp, xs, w)


def _norm_two_sink_kernel(x_ref, w_ref, op_ref, os_ref, *, n_p):
    i = pl.program_id(0)

    @pl.when(i < n_p)
    def _():
        op_ref[...] = _rms_rows(x_ref[...], w_ref[...])

    @pl.when(i >= n_p)
    def _():
        os_ref[...] = _rms_rows(x_ref[...], w_ref[...])


def _norm_two_sink(x, w, mp, tr):
    m, d = x.shape
    ms = m - mp
    n_p, n_s = mp // tr, ms // tr
    return pl.pallas_call(
        functools.partial(_norm_two_sink_kernel, n_p=n_p),
        grid=(n_p + n_s,),
        in_specs=[pl.BlockSpec((tr, d), lambda i: (i, 0)), pl.BlockSpec((1, d), lambda i: (0, 0))],
        out_specs=[
            pl.BlockSpec((tr, d), lambda i: (jnp.minimum(i, n_p - 1), 0)),
            pl.BlockSpec((tr, d), lambda i: (jnp.maximum(i - n_p, 0), 0)),
        ],
        out_shape=[jax.ShapeDtypeStruct((mp, d), F32), jax.ShapeDtypeStruct((ms, d), F32)],
        compiler_params=_params(("arbitrary",)),
        name="final_norm",
    )(x, w)


def _cast_rows(r, n_steps):
    for rows in range(BF16_ROWS, r + 1, BF16_ROWS):
        if r % rows == 0 and r // rows <= n_steps:
            return rows
    raise ValueError(f"cannot cast {r} rows in {n_steps} grid steps")


def _in_proj_kernel(*refs, n_cast):
    a_ref, b_ref = refs[0:2]
    cast_in = refs[2:2 + n_cast]
    o_ref = refs[2 + n_cast]
    cast_out = refs[3 + n_cast:3 + 2 * n_cast]
    for src, dst in zip(cast_in, cast_out):
        dst[...] = src[...].astype(dst.dtype)

    acc = jnp.dot(a_ref[...], b_ref[...].astype(BF16), preferred_element_type=F32)
    for g in range(o_ref.shape[0]):
        o_ref[g, 0] = acc[:, g * LANE:(g + 1) * LANE]


def _in_proj(h, w, tm, tn, n_groups, casts):
    m, k = h.shape
    n = w.shape[1]
    ni, nj = m // tm, n // tn
    gpt = tn // LANE
    tps = n_groups // gpt
    in_specs = [pl.BlockSpec((tm, k), lambda i, j: (i, 0)), pl.BlockSpec((k, tn), lambda i, j: (0, j))]
    out_specs = [pl.BlockSpec((gpt, 1, tm, LANE), lambda i, j: (j % tps, j // tps, i, 0))]
    out_shape = [jax.ShapeDtypeStruct((n_groups, n // (n_groups * LANE), m, LANE), F32)]
    for arr in casts:
        r, c = arr.shape
        rows = _cast_rows(r, ni * nj)
        spec = pl.BlockSpec((rows, c), lambda i, j, last=r // rows - 1: (jnp.minimum(i * nj + j, last), 0))
        in_specs.append(spec)
        out_specs.append(spec)
        out_shape.append(jax.ShapeDtypeStruct((r, c), BF16))
    outs = pl.pallas_call(
        functools.partial(_in_proj_kernel, n_cast=len(casts)),
        grid=(ni, nj),
        in_specs=in_specs,
        out_specs=out_specs,
        out_shape=out_shape,
        compiler_params=_params(("arbitrary", "arbitrary")),
        name="in_proj",
    )(h, w, *casts)
    return outs[0], outs[1:]


STATE_LAG = 4
MIXER_BLOCKS_PER_STEP = 256


def _mixer_kernel(*refs, tc, hb, n_chunks, has_cache, layer):
    if has_cache:
        (p_ref, convw_ref, lbl_ref, hgn_ref, cbuf_ref, s0_ref,
         y_ref, nconv_ref, nst_ref, st_ref, cu_ref) = refs
    else:
        p_ref, convw_ref, lbl_ref, hgn_ref, y_ref, nconv_ref, nst_ref, st_ref, cu_ref = refs
    SEG_B, SEG_C, SEG_U, SEG_Q, SEG_FZ, SEG_I, SEG_OG = range(7)
    c = pl.program_id(2)

    @pl.when(c == 0)
    def _init():
        for hh in range(hb):
            if has_cache:
                st_ref[hh] = s0_ref[0, hh].T
                cu_ref[hh, SUBLANES - (CONV_W - 1):SUBLANES, :] = cbuf_ref[0, :, hh * LANE:(hh + 1) * LANE]
            else:
                st_ref[hh] = jnp.zeros((HEAD_DIM, HEAD_DIM), F32)
                cu_ref[hh, 0:SUBLANES, :] = jnp.zeros((SUBLANES, LANE), F32)

    hgn = hgn_ref[...]
    sub = lax.broadcasted_iota(jnp.int32, (SUBLANES, LANE), 0)
    tri = (lax.broadcasted_iota(jnp.int32, (SCAN_BLOCK, SCAN_BLOCK), 0)
           >= lax.broadcasted_iota(jnp.int32, (SCAN_BLOCK, SCAN_BLOCK), 1))
    mid = SCAN_BLOCK // 2
    nt = (((1,), (1,)), ((), ()))
    tn = (((0,), (0,)), ((), ()))

    def head_consts(hh):
        lanes = slice(hh * LANE, (hh + 1) * LANE)
        logits = lbl_ref[:, lanes]
        ex = jnp.exp(logits - jnp.max(logits, axis=0, keepdims=True))
        lb = jnp.sum(ex[0:layer + 1, :], axis=0, keepdims=True) / jnp.sum(ex, axis=0, keepdims=True)
        return lb, 1.0 - lb, convw_ref[0:1, lanes], convw_ref[1:2, lanes], convw_ref[2:3, lanes]

    consts = [head_consts(hh) for hh in range(hb)]

    def prefix8(x):
        sh = 1
        while sh < SUBLANES:
            x = x + jnp.where(sub >= sh, pltpu.roll(x, sh, 0), 0.0)
            sh *= 2
        return x

    def conv_block(hh, lo):
        _, _, w0, w1, w2 = consts[hh]
        rows = pl.ds(lo, SCAN_BLOCK)
        cu = p_ref[hh, SEG_C, rows, :] * p_ref[hh, SEG_U, rows, :]
        cu_ref[hh, pl.ds(SUBLANES + lo, SCAN_BLOCK), :] = cu
        s1 = cu_ref[hh, pl.ds(SUBLANES - 1 + lo, SCAN_BLOCK), :]
        s2 = cu_ref[hh, pl.ds(SUBLANES - 2 + lo, SCAN_BLOCK), :]
        conv = w0 * s2 + w1 * s1 + w2 * cu
        y_ref[0, rows, hh * LANE:(hh + 1) * LANE] = (p_ref[hh, SEG_B, rows, :] * conv).astype(y_ref.dtype)

    def block_terms(hh, lo):
        lb, one_m_lb = consts[hh][0:2]
        rows = pl.ds(lo, SCAN_BLOCK)
        f = lb + one_m_lb * jax.nn.sigmoid(p_ref[hh, SEG_FZ, rows, :])
        g2 = jnp.log2(f)
        k = 1.0 - f
        q = _silu(p_ref[hh, SEG_Q, rows, :])
        v = p_ref[hh, SEG_I, rows, :].astype(BF16)
        parts = []
        total = None
        for p in range(SCAN_BLOCK // SUBLANES):
            part = prefix8(g2[p * SUBLANES:(p + 1) * SUBLANES, :])
            if total is not None:
                part = part + total
            total = part[SUBLANES - 1:SUBLANES, :]
            parts.append(part)
        lc = jnp.concatenate(parts, axis=0)
        ref = lc[mid - 1:mid, :]
        qd = (q * jnp.exp2(lc - ref)).astype(BF16)
        kd = (k * jnp.exp2(ref - lc)).astype(BF16)
        att = lax.dot_general(qd, kd, nt, preferred_element_type=F32)
        att = jnp.where(tri, att, 0.0).astype(BF16)
        return q, k, v, lc, total, att

    def state_free(hh, lo):
        qa, ka, va, lca, lla, att_a = block_terms(hh, lo)
        qb, kb, vb, lcb, llb, att_b = block_terms(hh, lo + SCAN_BLOCK)
        qe_a = (qa * jnp.exp2(lca)).astype(BF16)
        qe_b = (qb * jnp.exp2(lcb)).astype(BF16)
        kdec_a = (ka * jnp.exp2(lla - lca)).astype(BF16)
        kdec_b = (kb * jnp.exp2(llb - lcb)).astype(BF16)
        cross = lax.dot_general(qe_b, kdec_a, nt, preferred_element_type=F32).astype(BF16)
        qe = jnp.concatenate([qe_a, (qb * jnp.exp2(lcb + lla)).astype(BF16)], axis=0)
        kend = jnp.concatenate([(ka * jnp.exp2(lla + llb - lca)).astype(BF16), kdec_b], axis=0)
        upd = lax.dot_general(jnp.concatenate([va, vb], axis=0), kend, tn,
                              preferred_element_type=F32)
        return (att_a, att_b, cross, va, vb), upd, jnp.exp2(lla + llb), qe

    pair = 2 * SCAN_BLOCK
    pairs = [(hh, j * pair) for hh in range(hb) for j in range(tc // pair)]
    st = [st_ref[hh] for hh in range(hb)]
    pending = {}
    for n in range(len(pairs) + STATE_LAG):
        if n < len(pairs):
            hh, lo = pairs[n]
            conv_block(hh, lo)
            conv_block(hh, lo + SCAN_BLOCK)
            pending[n] = state_free(hh, lo)
        if n >= STATE_LAG:
            hh, lo = pairs[n - STATE_LAG]
            (att_a, att_b, cross, va, vb), upd, decay, qe = pending.pop(n - STATE_LAG)
            o_a = jnp.dot(att_a, va, preferred_element_type=F32)
            o_b = (jnp.dot(att_b, vb, preferred_element_type=F32)
                   + jnp.dot(cross, va, preferred_element_type=F32))
            o = (jnp.concatenate([o_a, o_b], axis=0)
                 + lax.dot_general(qe, st[hh].astype(BF16), nt, preferred_element_type=F32))
            st[hh] = st[hh] * decay + upd
            rows = pl.ds(lo, pair)
            ms = jnp.mean(o * o, axis=-1, keepdims=True)
            yb = o * lax.rsqrt(ms + NORM_EPS) * (hgn * _silu(p_ref[hh, SEG_OG, rows, :]))
            y_ref[1, rows, hh * LANE:(hh + 1) * LANE] = yb.astype(y_ref.dtype)
    for hh in range(hb):
        st_ref[hh] = st[hh]
        tail = cu_ref[hh, SUBLANES + tc - (CONV_W - 1):SUBLANES + tc, :]
        cu_ref[hh, SUBLANES - (CONV_W - 1):SUBLANES, :] = tail
        nconv_ref[0, :, hh * LANE:(hh + 1) * LANE] = tail

    @pl.when(c == n_chunks - 1)
    def _fin():
        for hh in range(hb):
            nst_ref[0, hh] = st[hh].T


def _mixer(proj, conv_w, lb_logits, hg_norm, *, row0, nseq, t, tc, hb, n_groups, layer, cache=None, state=None):
    has_cache = cache is not None
    if has_cache != (state is not None):
        raise ValueError("conv cache and recurrent state come together")
    n_chunks = t // tc
    rb0 = row0 // tc
    n_hblk = n_groups // hb

    n_seg = proj.shape[1]
    in_specs = [
        pl.BlockSpec((hb, n_seg, tc, LANE), lambda b, h, c: (h, 0, rb0 + b * n_chunks + c, 0)),
        pl.BlockSpec((CONV_W, hb * LANE), lambda b, h, c: (0, h)),
        pl.BlockSpec((lb_logits.shape[0], hb * LANE), lambda b, h, c: (0, h)),
        pl.BlockSpec((1, LANE), lambda b, h, c: (0, 0)),
    ]
    args = [proj, conv_w, lb_logits, hg_norm]
    if has_cache:
        in_specs += [
            pl.BlockSpec((1, CONV_W - 1, hb * LANE), lambda b, h, c: (b, 0, h)),
            pl.BlockSpec((1, hb, HEAD_DIM, HEAD_DIM), lambda b, h, c: (b, h, 0, 0)),
        ]
        args += [cache, state]
    d_half = n_groups * LANE
    out_specs = [
        pl.BlockSpec((2, tc, hb * LANE), lambda b, h, c: (0, b * n_chunks + c, h)),
        pl.BlockSpec((1, CONV_W - 1, hb * LANE), lambda b, h, c: (b, 0, h)),
        pl.BlockSpec((1, hb, HEAD_DIM, HEAD_DIM), lambda b, h, c: (b, h, 0, 0)),
    ]
    out_shape = [
        jax.ShapeDtypeStruct((2, nseq * t, d_half), BF16),
        jax.ShapeDtypeStruct((nseq, CONV_W - 1, d_half), F32),
        jax.ShapeDtypeStruct((nseq, n_groups, HEAD_DIM, HEAD_DIM), F32),
    ]
    kern = functools.partial(_mixer_kernel, tc=tc, hb=hb, n_chunks=n_chunks, has_cache=has_cache,
                             layer=layer)
    return pl.pallas_call(
        kern,
        grid=(nseq, n_hblk, n_chunks),
        in_specs=in_specs,
        out_specs=out_specs,
        out_shape=out_shape,
        scratch_shapes=[pltpu.VMEM((hb, HEAD_DIM, HEAD_DIM), F32),
                        pltpu.VMEM((hb, SUBLANES + tc, LANE), F32)],
        compiler_params=_params(("parallel", "parallel", "arbitrary")),
        name="mixer_cached" if has_cache else "mixer_prompt",
    )(*args)


def _out_proj_kernel(yp_ref, ys_ref, w_ref, xp_ref, xs_ref, wn_ref, o_ref, xw_ref, ssq_ref, *, n_p):
    i = pl.program_id(1)

    def emit(y_ref, x_ref):
        y = jnp.concatenate([y_ref[0], y_ref[1]], axis=1)
        x1 = x_ref[...] + jnp.dot(y, w_ref[...], preferred_element_type=F32)
        o_ref[...] = x1
        xw_ref[...] = (x1 * wn_ref[...]).astype(xw_ref.dtype)
        sq = x1 * x1
        acc = sq[:, 0:LANE]
        for g in range(1, sq.shape[1] // LANE):
            acc = acc + sq[:, g * LANE:(g + 1) * LANE]
        ssq_ref[...] = acc

    @pl.when(i < n_p)
    def _():
        emit(yp_ref, xp_ref)

    @pl.when(i >= n_p)
    def _():
        emit(ys_ref, xs_ref)


def _out_proj(y_p, y_s, w, xp, xs, w_norm, tm, tn):
    _, mp, dh = y_p.shape
    ms = y_s.shape[1]
    kdim, n = w.shape
    n_p, n_s = mp // tm, ms // tm
    m = mp + ms

    def prow(j, i):
        return jnp.minimum(i, n_p - 1)

    def srow(j, i):
        return jnp.maximum(i - n_p, 0)

    return pl.pallas_call(
        functools.partial(_out_proj_kernel, n_p=n_p),
        grid=(n // tn, n_p + n_s),
        in_specs=[
            pl.BlockSpec((2, tm, dh), lambda j, i: (0, prow(j, i), 0)),
            pl.BlockSpec((2, tm, dh), lambda j, i: (0, srow(j, i), 0)),
            pl.BlockSpec((kdim, tn), lambda j, i: (0, j)),
            pl.BlockSpec((tm, tn), lambda j, i: (prow(j, i), j)),
            pl.BlockSpec((tm, tn), lambda j, i: (srow(j, i), j)),
            pl.BlockSpec((1, tn), lambda j, i: (0, j)),
        ],
        out_specs=[
            pl.BlockSpec((tm, tn), lambda j, i: (i, j)),
            pl.BlockSpec((tm, tn), lambda j, i: (i, j)),
            pl.BlockSpec((tm, LANE), lambda j, i: (i, j)),
        ],
        out_shape=[
            jax.ShapeDtypeStruct((m, n), F32),
            jax.ShapeDtypeStruct((m, n), BF16),
            jax.ShapeDtypeStruct((m, (n // tn) * LANE), F32),
        ],
        compiler_params=_params(("parallel", "parallel")),
        name="out_proj",
    )(y_p, y_s, w, xp, xs, w_norm)


FFN_UP_SUB_ROWS = 256


def _ffn_up_kernel(a_ref, ssq_ref, wg_ref, wu_ref, o_ref, *, sub_rows):
    wg = wg_ref[...].astype(BF16)
    wu = wu_ref[...].astype(BF16)
    for r0 in range(0, a_ref.shape[0], sub_rows):
        rows = pl.ds(r0, sub_rows)
        a = a_ref[rows, :]
        r = lax.rsqrt(jnp.sum(ssq_ref[rows, :], axis=-1, keepdims=True) / a.shape[1] + NORM_EPS)
        g = r * jnp.dot(a, wg, preferred_element_type=F32)
        u = r * jnp.dot(a, wu, preferred_element_type=F32)
        o_ref[rows, :] = (_silu(g) * u).astype(o_ref.dtype)


def _ffn_up(xw, ssq, wg, wu, tm, tn):
    m, k = xw.shape
    n = wg.shape[1]
    wspec = pl.BlockSpec((k, tn), lambda j, i: (0, j))
    return pl.pallas_call(
        functools.partial(_ffn_up_kernel, sub_rows=_largest_tile(tm, FFN_UP_SUB_ROWS, BF16_ROWS)),
        grid=(n // tn, m // tm),
        in_specs=[
            pl.BlockSpec((tm, k), lambda j, i: (i, 0)),
            pl.BlockSpec((tm, ssq.shape[1]), lambda j, i: (i, 0)),
            wspec, wspec,
        ],
        out_specs=pl.BlockSpec((tm, tn), lambda j, i: (i, j)),
        out_shape=jax.ShapeDtypeStruct((m, n), BF16),
        compiler_params=_params(("parallel", "parallel")),
        name="ffn_up",
    )(xw, ssq, wg, wu)


def _ffn_down_kernel(a_ref, w_ref, x_ref, o_ref):
    o_ref[...] = x_ref[...] + jnp.dot(a_ref[...], w_ref[...], preferred_element_type=F32)


def _ffn_down(act, w, x1, tm, tn):
    m, kf = act.shape
    n = w.shape[1]
    return pl.pallas_call(
        _ffn_down_kernel,
        grid=(n // tn, m // tm),
        in_specs=[
            pl.BlockSpec((tm, kf), lambda j, i: (i, 0)),
            pl.BlockSpec((kf, tn), lambda j, i: (0, j)),
            pl.BlockSpec((tm, tn), lambda j, i: (i, j)),
        ],
        out_specs=pl.BlockSpec((tm, tn), lambda j, i: (i, j)),
        out_shape=jax.ShapeDtypeStruct((m, n), F32),
        compiler_params=_params(("parallel", "parallel")),
        name="ffn_down",
    )(act, w, x1)


def _mixer_tiles(t, n_groups):
    tc = _largest_tile(t, MIXER_BLOCKS_PER_STEP * SCAN_BLOCK, 2 * SCAN_BLOCK)
    hb = 1
    while hb < n_groups and 2 * hb * tc <= MIXER_BLOCKS_PER_STEP * SCAN_BLOCK and n_groups % (2 * hb) == 0:
        hb *= 2
    return dict(tc=tc, hb=hb)


def _tiles(mp, ms, tp, ts, n_groups, d, n_in, d_ff):
    m = mp + ms
    tr = _largest_tile(_gcd(mp, ms), 512, BF16_ROWS)
    return dict(
        tr=tr,
        tm=_largest_tile(m, 1536, BF16_ROWS),
        tn_in=_largest_tile(n_in, 512, LANE),
        tn_out=_largest_tile(d, 1024, LANE),
        tn_up=_largest_tile(d_ff, 256, LANE),
        tm_down=_largest_tile(m, 704, BF16_ROWS),
        tn_down=_largest_tile(d, 512, LANE),
        mix_p=_mixer_tiles(tp, n_groups),
        mix_s=_mixer_tiles(ts, n_groups),
    )


def kernel(x_prompt, x_sample, cache_conv, state_hgrn, norm_mix, w_in, conv_w, lb_logits, hg_norm,
           w_out, norm_ffn, w_gate, w_up, w_down, norm_final):
    bp, tp, d = x_prompt.shape
    bs, ts, _ = x_sample.shape
    depth = w_in.shape[0]
    if depth != 1:
        raise NotImplementedError("one layer")
    layer = 0
    n_in = w_in.shape[2]
    d_ff = w_gate.shape[2]
    d_half = conv_w.shape[2]
    n_groups = d_half // LANE
    if n_in != 7 * d_half or state_hgrn.shape[2] != n_groups or w_out.shape[1] != 2 * d_half:
        raise ValueError("unexpected layer geometry")
    mp, ms = bp * tp, bs * ts
    m = mp + ms
    t = _tiles(mp, ms, tp, ts, n_groups, d, n_in, d_ff)

    xp = x_prompt.reshape(mp, d)
    xs = x_sample.reshape(ms, d)
    h = _norm_two_src(xp, xs, norm_mix[layer].reshape(1, d), t["tr"])
    proj, (w_out_b, w_down_b) = _in_proj(h, w_in[layer], t["tm"], t["tn_in"], n_groups,
                                         casts=[w_out[layer], w_down[layer]])

    mix_args = dict(n_groups=n_groups, layer=layer)
    hgn = hg_norm[layer].reshape(1, HEAD_DIM)
    y_p, conv_p, hg_p = _mixer(proj, conv_w[layer], lb_logits, hgn, row0=0, nseq=bp, t=tp,
                                      **t["mix_p"], **mix_args)
    y_s, conv_s, hg_s = _mixer(proj, conv_w[layer], lb_logits, hgn, row0=mp, nseq=bs, t=ts,
                                      cache=cache_conv[layer], state=state_hgrn[layer],
                                      **t["mix_s"], **mix_args)

    x1, x1w, ssq = _out_proj(y_p, y_s, w_out_b, xp, xs, norm_ffn[layer].reshape(1, d),
                             t["tr"], t["tn_out"])
    act = _ffn_up(x1w, ssq, w_gate[layer], w_up[layer], t["tm"], t["tn_up"])
    x2 = _ffn_down(act, w_down_b, x1, t["tm_down"], t["tn_down"])
    y_p, y_s = _norm_two_sink(x2, norm_final.reshape(1, d), mp, t["tr"])

    return (y_p.reshape(bp, tp, d), y_s.reshape(bs, ts, d),
            conv_p[None], hg_p[None].astype(x_prompt.dtype),
            conv_s[None].astype(cache_conv.dtype), hg_s[None].astype(state_hgrn.dtype))
```

```python
import functools

import jax
import jax.numpy as jnp
from jax import lax
from jax.experimental import pallas as pl
from jax.experimental.pallas import tpu as pltpu

LANE = 128
SUBLANES = 8
BF16_ROWS = 16
SCAN_BLOCK = 16
HEAD_DIM = 128
CONV_W = 3
NORM_EPS = 1e-6
V7X_VMEM_BYTES = 64 * 1024 * 1024
VMEM_LIMIT = V7X_VMEM_BYTES - 3 * 1024 * 1024

F32 = jnp.float32
BF16 = jnp.bfloat16


def _largest_tile(n, cap, quantum):
    best = None
    for t in range(quantum, min(n, cap) + 1, quantum):
        if n % t == 0:
            best = t
    if best is None:
        raise ValueError(f"no tile for n={n} cap={cap} quantum={quantum}")
    return best


def _gcd(a, b):
    while b:
        a, b = b, a % b
    return a


def _params(sem):
    return pltpu.CompilerParams(dimension_semantics=sem, vmem_limit_bytes=VMEM_LIMIT)


def _rms_rows(x, w):
    ms = jnp.mean(x * x, axis=-1, keepdims=True)
    return x * lax.rsqrt(ms + NORM_EPS) * w


def _silu(x):
    return x * jax.nn.sigmoid(x)


def _norm_two_src_kernel(xp_ref, xs_ref, w_ref, o_ref, *, n_p):
    i = pl.program_id(0)

    @pl.when(i < n_p)
    def _():
        o_ref[...] = _rms_rows(xp_ref[...], w_ref[...]).astype(o_ref.dtype)

    @pl.when(i >= n_p)
    def _():
        o_ref[...] = _rms_rows(xs_ref[...], w_ref[...]).astype(o_ref.dtype)


def _norm_two_src(xp, xs, w, tr):
    mp, d = xp.shape
    ms = xs.shape[0]
    n_p, n_s = mp // tr, ms // tr
    return pl.pallas_call(
        functools.partial(_norm_two_src_kernel, n_p=n_p),
        grid=(n_p + n_s,),
        in_specs=[
            pl.BlockSpec((tr, d), lambda i: (jnp.minimum(i, n_p - 1), 0)),
            pl.BlockSpec((tr, d), lambda i: (jnp.maximum(i - n_p, 0), 0)),
            pl.BlockSpec((1, d), lambda i: (0, 0)),
        ],
        out_specs=pl.BlockSpec((tr, d), lambda i: (i, 0)),
        out_shape=jax.ShapeDtypeStruct((mp + ms, d), BF16),
        compiler_params=_params(("parallel",)),
        name="norm1",
    )(xp, xs, w)


def _norm_two_sink_kernel(x_ref, w_ref, op_ref, os_ref, *, n_p):
    i = pl.program_id(0)

    @pl.when(i < n_p)
    def _():
        op_ref[...] = _rms_rows(x_ref[...], w_ref[...])

    @pl.when(i >= n_p)
    def _():
        os_ref[...] = _rms_rows(x_ref[...], w_ref[...])


def _norm_two_sink(x, w, mp, tr):
    m, d = x.shape
    ms = m - mp
    n_p, n_s = mp // tr, ms // tr
    return pl.pallas_call(
        functools.partial(_norm_two_sink_kernel, n_p=n_p),
        grid=(n_p + n_s,),
        in_specs=[pl.BlockSpec((tr, d), lambda i: (i, 0)), pl.BlockSpec((1, d), lambda i: (0, 0))],
        out_specs=[
            pl.BlockSpec((tr, d), lambda i: (jnp.minimum(i, n_p - 1), 0)),
            pl.BlockSpec((tr, d), lambda i: (jnp.maximum(i - n_p, 0), 0)),
        ],
        out_shape=[jax.ShapeDtypeStruct((mp, d), F32), jax.ShapeDtypeStruct((ms, d), F32)],
        compiler_params=_params(("arbitrary",)),
        name="final_norm",
    )(x, w)


def _cast_rows(r, n_steps):
    for rows in range(BF16_ROWS, r + 1, BF16_ROWS):
        if r % rows == 0 and r // rows <= n_steps:
            return rows
    raise ValueError(f"cannot cast {r} rows in {n_steps} grid steps")


def _in_proj_kernel(*refs, n_cast):
    a_ref, b_ref = refs[0:2]
    cast_in = refs[2:2 + n_cast]
    o_ref = refs[2 + n_cast]
    cast_out = refs[3 + n_cast:3 + 2 * n_cast]
    for src, dst in zip(cast_in, cast_out):
        dst[...] = src[...].astype(dst.dtype)

    acc = jnp.dot(a_ref[...], b_ref[...].astype(BF16), preferred_element_type=F32)
    for g in range(o_ref.shape[0]):
        o_ref[g, 0] = acc[:, g * LANE:(g + 1) * LANE]


def _in_proj(h, w, tm, tn, n_groups, casts):
    m, k = h.shape
    n = w.shape[1]
    ni, nj = m // tm, n // tn
    gpt = tn // LANE
    tps = n_groups // gpt
    in_specs = [pl.BlockSpec((tm, k), lambda i, j: (i, 0)), pl.BlockSpec((k, tn), lambda i, j: (0, j))]
    out_specs = [pl.BlockSpec((gpt, 1, tm, LANE), lambda i, j: (j % tps, j // tps, i, 0))]
    out_shape = [jax.ShapeDtypeStruct((n_groups, n // (n_groups * LANE), m, LANE), F32)]
    for arr in casts:
        r, c = arr.shape
        rows = _cast_rows(r, ni * nj)
        spec = pl.BlockSpec((rows, c), lambda i, j, last=r // rows - 1: (jnp.minimum(i * nj + j, last), 0))
        in_specs.append(spec)
        out_specs.append(spec)
        out_shape.append(jax.ShapeDtypeStruct((r, c), BF16))
    outs = pl.pallas_call(
        functools.partial(_in_proj_kernel, n_cast=len(casts)),
        grid=(ni, nj),
        in_specs=in_specs,
        out_specs=out_specs,
        out_shape=out_shape,
        compiler_params=_params(("arbitrary", "arbitrary")),
        name="in_proj",
    )(h, w, *casts)
    return outs[0], outs[1:]


STATE_LAG = 6
MIXER_BLOCKS_PER_STEP = 256


def _mixer_kernel(*refs, tc, hb, n_chunks, has_cache, layer):
    if has_cache:
        (p_ref, convw_ref, lbl_ref, hgn_ref, cbuf_ref, s0_ref,
         y_ref, nconv_ref, nst_ref, st_ref, cu_ref) = refs
    else:
        p_ref, convw_ref, lbl_ref, hgn_ref, y_ref, nconv_ref, nst_ref, st_ref, cu_ref = refs
    SEG_B, SEG_C, SEG_U, SEG_Q, SEG_FZ, SEG_I, SEG_OG = range(7)
    c = pl.program_id(2)

    @pl.when(c == 0)
    def _init():
        for hh in range(hb):
            if has_cache:
                st_ref[hh] = s0_ref[0, hh].T
                cu_ref[hh, SUBLANES - (CONV_W - 1):SUBLANES, :] = cbuf_ref[0, :, hh * LANE:(hh + 1) * LANE]
            else:
                st_ref[hh] = jnp.zeros((HEAD_DIM, HEAD_DIM), F32)
                cu_ref[hh, 0:SUBLANES, :] = jnp.zeros((SUBLANES, LANE), F32)

    hgn = hgn_ref[...]
    sub = lax.broadcasted_iota(jnp.int32, (SUBLANES, LANE), 0)
    tri = (lax.broadcasted_iota(jnp.int32, (SCAN_BLOCK, SCAN_BLOCK), 0)
           >= lax.broadcasted_iota(jnp.int32, (SCAN_BLOCK, SCAN_BLOCK), 1))
    mid = SCAN_BLOCK // 2
    nt = (((1,), (1,)), ((), ()))
    tn = (((0,), (0,)), ((), ()))

    def head_consts(hh):
        lanes = slice(hh * LANE, (hh + 1) * LANE)
        logits = lbl_ref[:, lanes]
        ex = jnp.exp(logits - jnp.max(logits, axis=0, keepdims=True))
        lb = jnp.sum(ex[0:layer + 1, :], axis=0, keepdims=True) / jnp.sum(ex, axis=0, keepdims=True)
        return lb, 1.0 - lb, convw_ref[0:1, lanes], convw_ref[1:2, lanes], convw_ref[2:3, lanes]

    consts = [head_consts(hh) for hh in range(hb)]

    def prefix8(x):
        sh = 1
        while sh < SUBLANES:
            x = x + jnp.where(sub >= sh, pltpu.roll(x, sh, 0), 0.0)
            sh *= 2
        return x

    def conv_block(hh, lo):
        _, _, w0, w1, w2 = consts[hh]
        rows = pl.ds(lo, SCAN_BLOCK)
        cu = p_ref[hh, SEG_C, rows, :] * p_ref[hh, SEG_U, rows, :]
        cu_ref[hh, pl.ds(SUBLANES + lo, SCAN_BLOCK), :] = cu
        s1 = cu_ref[hh, pl.ds(SUBLANES - 1 + lo, SCAN_BLOCK), :]
        s2 = cu_ref[hh, pl.ds(SUBLANES - 2 + lo, SCAN_BLOCK), :]
        conv = w0 * s2 + w1 * s1 + w2 * cu
        y_ref[0, rows, hh * LANE:(hh + 1) * LANE] = (p_ref[hh, SEG_B, rows, :] * conv).astype(y_ref.dtype)

    def block_terms(hh, lo):
        lb, one_m_lb = consts[hh][0:2]
        rows = pl.ds(lo, SCAN_BLOCK)
        f = lb + one_m_lb * jax.nn.sigmoid(p_ref[hh, SEG_FZ, rows, :])
        g2 = jnp.log2(f)
        k = 1.0 - f
        q = _silu(p_ref[hh, SEG_Q, rows, :])
        v = p_ref[hh, SEG_I, rows, :].astype(BF16)
        parts = []
        total = None
        for p in range(SCAN_BLOCK // SUBLANES):
            part = prefix8(g2[p * SUBLANES:(p + 1) * SUBLANES, :])
            if total is not None:
                part = part + total
            total = part[SUBLANES - 1:SUBLANES, :]
            parts.append(part)
        lc = jnp.concatenate(parts, axis=0)
        ref = lc[mid - 1:mid, :]
        qd = (q * jnp.exp2(lc - ref)).astype(BF16)
        kd = (k * jnp.exp2(ref - lc)).astype(BF16)
        att = lax.dot_general(qd, kd, nt, preferred_element_type=F32)
        att = jnp.where(tri, att, 0.0).astype(BF16)
        return q, k, v, lc, total, att

    def state_free(hh, lo):
        qa, ka, va, lca, lla, att_a = block_terms(hh, lo)
        qb, kb, vb, lcb, llb, att_b = block_terms(hh, lo + SCAN_BLOCK)
        qe_a = (qa * jnp.exp2(lca)).astype(BF16)
        qe_b = (qb * jnp.exp2(lcb)).astype(BF16)
        kdec_a = (ka * jnp.exp2(lla - lca)).astype(BF16)
        kdec_b = (kb * jnp.exp2(llb - lcb)).astype(BF16)
        cross = lax.dot_general(qe_b, kdec_a, nt, preferred_element_type=F32).astype(BF16)
        qe = jnp.concatenate([qe_a, (qb * jnp.exp2(lcb + lla)).astype(BF16)], axis=0)
        kend = jnp.concatenate([(ka * jnp.exp2(lla + llb - lca)).astype(BF16), kdec_b], axis=0)
        upd = lax.dot_general(jnp.concatenate([va, vb], axis=0), kend, tn,
                              preferred_element_type=F32)
        return (att_a, att_b, cross, va, vb), upd, jnp.exp2(lla + llb), qe

    pair = 2 * SCAN_BLOCK
    pairs = [(hh, j * pair) for hh in range(hb) for j in range(tc // pair)]
    st = [st_ref[hh] for hh in range(hb)]
    pending = {}
    for n in range(len(pairs) + STATE_LAG):
        if n < len(pairs):
            hh, lo = pairs[n]
            conv_block(hh, lo)
            conv_block(hh, lo + SCAN_BLOCK)
            pending[n] = state_free(hh, lo)
        if n >= STATE_LAG:
            hh, lo = pairs[n - STATE_LAG]
            (att_a, att_b, cross, va, vb), upd, decay, qe = pending.pop(n - STATE_LAG)
            o_a = jnp.dot(att_a, va, preferred_element_type=F32)
            o_b = (jnp.dot(att_b, vb, preferred_element_type=F32)
                   + jnp.dot(cross, va, preferred_element_type=F32))
            o = (jnp.concatenate([o_a, o_b], axis=0)
                 + lax.dot_general(qe, st[hh].astype(BF16), nt, preferred_element_type=F32))
            st[hh] = st[hh] * decay + upd
            rows = pl.ds(lo, pair)
            ms = jnp.mean(o * o, axis=-1, keepdims=True)
            yb = o * lax.rsqrt(ms + NORM_EPS) * (hgn * _silu(p_ref[hh, SEG_OG, rows, :]))
            y_ref[1, rows, hh * LANE:(hh + 1) * LANE] = yb.astype(y_ref.dtype)
    for hh in range(hb):
        st_ref[hh] = st[hh]
        tail = cu_ref[hh, SUBLANES + tc - (CONV_W - 1):SUBLANES + tc, :]
        cu_ref[hh, SUBLANES - (CONV_W - 1):SUBLANES, :] = tail
        nconv_ref[0, :, hh * LANE:(hh + 1) * LANE] = tail

    @pl.when(c == n_chunks - 1)
    def _fin():
        for hh in range(hb):
            nst_ref[0, hh] = st[hh].T


def _mixer(proj, conv_w, lb_logits, hg_norm, *, row0, nseq, t, tc, hb, n_groups, layer, cache=None, state=None):
    has_cache = cache is not None
    if has_cache != (state is not None):
        raise ValueError("conv cache and recurrent state come together")
    n_chunks = t // tc
    rb0 = row0 // tc
    n_hblk = n_groups // hb

    n_seg = proj.shape[1]
    in_specs = [
        pl.BlockSpec((hb, n_seg, tc, LANE), lambda b, h, c: (h, 0, rb0 + b * n_chunks + c, 0)),
        pl.BlockSpec((CONV_W, hb * LANE), lambda b, h, c: (0, h)),
        pl.BlockSpec((lb_logits.shape[0], hb * LANE), lambda b, h, c: (0, h)),
        pl.BlockSpec((1, LANE), lambda b, h, c: (0, 0)),
    ]
    args = [proj, conv_w, lb_logits, hg_norm]
    if has_cache:
        in_specs += [
            pl.BlockSpec((1, CONV_W - 1, hb * LANE), lambda b, h, c: (b, 0, h)),
            pl.BlockSpec((1, hb, HEAD_DIM, HEAD_DIM), lambda b, h, c: (b, h, 0, 0)),
        ]
        args += [cache, state]
    d_half = n_groups * LANE
    out_specs = [
        pl.BlockSpec((2, tc, hb * LANE), lambda b, h, c: (0, b * n_chunks + c, h)),
        pl.BlockSpec((1, CONV_W - 1, hb * LANE), lambda b, h, c: (b, 0, h)),
        pl.BlockSpec((1, hb, HEAD_DIM, HEAD_DIM), lambda b, h, c: (b, h, 0, 0)),
    ]
    out_shape = [
        jax.ShapeDtypeStruct((2, nseq * t, d_half), BF16),
        jax.ShapeDtypeStruct((nseq, CONV_W - 1, d_half), F32),
        jax.ShapeDtypeStruct((nseq, n_groups, HEAD_DIM, HEAD_DIM), F32),
    ]
    kern = functools.partial(_mixer_kernel, tc=tc, hb=hb, n_chunks=n_chunks, has_cache=has_cache,
                             layer=layer)
    return pl.pallas_call(
        kern,
        grid=(nseq, n_hblk, n_chunks),
        in_specs=in_specs,
        out_specs=out_specs,
        out_shape=out_shape,
        scratch_shapes=[pltpu.VMEM((hb, HEAD_DIM, HEAD_DIM), F32),
                        pltpu.VMEM((hb, SUBLANES + tc, LANE), F32)],
        compiler_params=_params(("parallel", "parallel", "arbitrary")),
        name="mixer_cached" if has_cache else "mixer_prompt",
    )(*args)


def _out_proj_kernel(yp_ref, ys_ref, w_ref, xp_ref, xs_ref, wn_ref, o_ref, xw_ref, ssq_ref, *, n_p):
    i = pl.program_id(1)

    def emit(y_ref, x_ref):
        y = jnp.concatenate([y_ref[0], y_ref[1]], axis=1)
        x1 = x_ref[...] + jnp.dot(y, w_ref[...], preferred_element_type=F32)
        o_ref[...] = x1
        xw_ref[...] = (x1 * wn_ref[...]).astype(xw_ref.dtype)
        sq = x1 * x1
        acc = sq[:, 0:LANE]
        for g in range(1, sq.shape[1] // LANE):
            acc = acc + sq[:, g * LANE:(g + 1) * LANE]
        ssq_ref[...] = acc

    @pl.when(i < n_p)
    def _():
        emit(yp_ref, xp_ref)

    @pl.when(i >= n_p)
    def _():
        emit(ys_ref, xs_ref)


def _out_proj(y_p, y_s, w, xp, xs, w_norm, tm, tn):
    _, mp, dh = y_p.shape
    ms = y_s.shape[1]
    kdim, n = w.shape
    n_p, n_s = mp // tm, ms // tm
    m = mp + ms

    def prow(j, i):
        return jnp.minimum(i, n_p - 1)

    def srow(j, i):
        return jnp.maximum(i - n_p, 0)

    return pl.pallas_call(
        functools.partial(_out_proj_kernel, n_p=n_p),
        grid=(n // tn, n_p + n_s),
        in_specs=[
            pl.BlockSpec((2, tm, dh), lambda j, i: (0, prow(j, i), 0)),
            pl.BlockSpec((2, tm, dh), lambda j, i: (0, srow(j, i), 0)),
            pl.BlockSpec((kdim, tn), lambda j, i: (0, j)),
            pl.BlockSpec((tm, tn), lambda j, i: (prow(j, i), j)),
            pl.BlockSpec((tm, tn), lambda j, i: (srow(j, i), j)),
            pl.BlockSpec((1, tn), lambda j, i: (0, j)),
        ],
        out_specs=[
            pl.BlockSpec((tm, tn), lambda j, i: (i, j)),
            pl.BlockSpec((tm, tn), lambda j, i: (i, j)),
            pl.BlockSpec((tm, LANE), lambda j, i: (i, j)),
        ],
        out_shape=[
            jax.ShapeDtypeStruct((m, n), F32),
            jax.ShapeDtypeStruct((m, n), BF16),
            jax.ShapeDtypeStruct((m, (n // tn) * LANE), F32),
        ],
        compiler_params=_params(("parallel", "parallel")),
        name="out_proj",
    )(y_p, y_s, w, xp, xs, w_norm)


FFN_UP_SUB_ROWS = 256


def _ffn_up_kernel(a_ref, ssq_ref, wg_ref, wu_ref, o_ref, *, sub_rows):
    wg = wg_ref[...].astype(BF16)
    wu = wu_ref[...].astype(BF16)
    for r0 in range(0, a_ref.shape[0], sub_rows):
        rows = pl.ds(r0, sub_rows)
        a = a_ref[rows, :]
        r = lax.rsqrt(jnp.sum(ssq_ref[rows, :], axis=-1, keepdims=True) / a.shape[1] + NORM_EPS)
        g = r * jnp.dot(a, wg, preferred_element_type=F32)
        u = r * jnp.dot(a, wu, preferred_element_type=F32)
        o_ref[rows, :] = (_silu(g) * u).astype(o_ref.dtype)


def _ffn_up(xw, ssq, wg, wu, tm, tn):
    m, k = xw.shape
    n = wg.shape[1]
    wspec = pl.BlockSpec((k, tn), lambda j, i: (0, j))
    return pl.pallas_call(
        functools.partial(_ffn_up_kernel, sub_rows=_largest_tile(tm, FFN_UP_SUB_ROWS, BF16_ROWS)),
        grid=(n // tn, m // tm),
        in_specs=[
            pl.BlockSpec((tm, k), lambda j, i: (i, 0)),
            pl.BlockSpec((tm, ssq.shape[1]), lambda j, i: (i, 0)),
            wspec, wspec,
        ],
        out_specs=pl.BlockSpec((tm, tn), lambda j, i: (i, j)),
        out_shape=jax.ShapeDtypeStruct((m, n), BF16),
        compiler_params=_params(("parallel", "parallel")),
        name="ffn_up",
    )(xw, ssq, wg, wu)


def _ffn_down_kernel(a_ref, w_ref, x_ref, o_ref):
    o_ref[...] = x_ref[...] + jnp.dot(a_ref[...], w_ref[...], preferred_element_type=F32)


def _ffn_down(act, w, x1, tm, tn):
    m, kf = act.shape
    n = w.shape[1]
    return pl.pallas_call(
        _ffn_down_kernel,
        grid=(n // tn, m // tm),
        in_specs=[
            pl.BlockSpec((tm, kf), lambda j, i: (i, 0)),
            pl.BlockSpec((kf, tn), lambda j, i: (0, j)),
            pl.BlockSpec((tm, tn), lambda j, i: (i, j)),
        ],
        out_specs=pl.BlockSpec((tm, tn), lambda j, i: (i, j)),
        out_shape=jax.ShapeDtypeStruct((m, n), F32),
        compiler_params=_params(("parallel", "parallel")),
        name="ffn_down",
    )(act, w, x1)


def _mixer_tiles(t, n_groups):
    tc = _largest_tile(t, MIXER_BLOCKS_PER_STEP * SCAN_BLOCK, 2 * SCAN_BLOCK)
    hb = 1
    while hb < n_groups and 2 * hb * tc <= MIXER_BLOCKS_PER_STEP * SCAN_BLOCK and n_groups % (2 * hb) == 0:
        hb *= 2
    return dict(tc=tc, hb=hb)


def _tiles(mp, ms, tp, ts, n_groups, d, n_in, d_ff):
    m = mp + ms
    tr = _largest_tile(_gcd(mp, ms), 512, BF16_ROWS)
    return dict(
        tr=tr,
        tm=_largest_tile(m, 1536, BF16_ROWS),
        tn_in=_largest_tile(n_in, 512, LANE),
        tn_out=_largest_tile(d, 1024, LANE),
        tn_up=_largest_tile(d_ff, 256, LANE),
        tm_down=_largest_tile(m, 704, BF16_ROWS),
        tn_down=_largest_tile(d, 512, LANE),
        mix_p=_mixer_tiles(tp, n_groups),
        mix_s=_mixer_tiles(ts, n_groups),
    )


def kernel(x_prompt, x_sample, cache_conv, state_hgrn, norm_mix, w_in, conv_w, lb_logits, hg_norm,
           w_out, norm_ffn, w_gate, w_up, w_down, norm_final):
    bp, tp, d = x_prompt.shape
    bs, ts, _ = x_sample.shape
    depth = w_in.shape[0]
    if depth != 1:
        raise NotImplementedError("one layer")
    layer = 0
    n_in = w_in.shape[2]
    d_ff = w_gate.shape[2]
    d_half = conv_w.shape[2]
    n_groups = d_half // LANE
    if n_in != 7 * d_half or state_hgrn.shape[2] != n_groups or w_out.shape[1] != 2 * d_half:
        raise ValueError("unexpected layer geometry")
    mp, ms = bp * tp, bs * ts
    m = mp + ms
    t = _tiles(mp, ms, tp, ts, n_groups, d, n_in, d_ff)

    xp = x_prompt.reshape(mp, d)
    xs = x_sample.reshape(ms, d)
    h = _norm_two_src(xp, xs, norm_mix[layer].reshape(1, d), t["tr"])
    proj, (w_out_b, w_down_b) = _in_proj(h, w_in[layer], t["tm"], t["tn_in"], n_groups,
                                         casts=[w_out[layer], w_down[layer]])

    mix_args = dict(n_groups=n_groups, layer=layer)
    hgn = hg_norm[layer].reshape(1, HEAD_DIM)
    y_p, conv_p, hg_p = _mixer(proj, conv_w[layer], lb_logits, hgn, row0=0, nseq=bp, t=tp,
                                      **t["mix_p"], **mix_args)
    y_s, conv_s, hg_s = _mixer(proj, conv_w[layer], lb_logits, hgn, row0=mp, nseq=bs, t=ts,
                                      cache=cache_conv[layer], state=state_hgrn[layer],
                                      **t["mix_s"], **mix_args)

    x1, x1w, ssq = _out_proj(y_p, y_s, w_out_b, xp, xs, norm_ffn[layer].reshape(1, d),
                             t["tr"], t["tn_out"])
    act = _ffn_up(x1w, ssq, w_gate[layer], w_up[layer], t["tm"], t["tn_up"])
    x2 = _ffn_down(act, w_down_b, x1, t["tm_down"], t["tn_down"])
    y_p, y_s = _norm_two_sink(x2, norm_final.reshape(1, d), mp, t["tr"])

    return (y_p.reshape(bp, tp, d), y_s.reshape(bs, ts, d),
            conv_p[None], hg_p[None].astype(x_prompt.dtype),
            conv_s[None].astype(cache_conv.dtype), hg_s[None].astype(state_hgrn.dtype))
```
